```python
import math
import jax, jax.numpy as jnp
from jax import lax
import numpy as np


D_MODEL = 1024
BATCH = 2
SEQ = 8192
DEPTH = 4
DEC_BATCH = 128
DEC_SEQ = 1
PAST_LEN = 2048
PAGE_SIZE = 128

N_A_LAYERS = DEPTH // 2
N_B_LAYERS = DEPTH - N_A_LAYERS
CONV_EXPAND = 2
E_A = CONV_EXPAND * D_MODEL
CONV_W = 31
HEAD_DIM = 64
N_HEADS = D_MODEL // HEAD_DIM
ATT_W = N_HEADS * HEAD_DIM
MOBA_BLOCK = 256
MOBA_TOPK = 3
Q_CHUNK = 128
EPS = 1e-6

kernel_name = 'yoco_conformer_moba_decoder_step'

F32 = jnp.float32


def rms_norm(x, g):
    xf = x.astype(F32)
    y = xf * lax.rsqrt(jnp.mean(xf * xf, axis=-1, keepdims=True) + EPS)
    return (y * g.astype(F32)).astype(x.dtype)


def layer_norm(x, g, b):
    xf = x.astype(F32)
    mu = jnp.mean(xf, axis=-1, keepdims=True)
    var = jnp.mean(jnp.square(xf - mu), axis=-1, keepdims=True)
    y = (xf - mu) * lax.rsqrt(var + EPS)
    return (y * g.astype(F32) + b.astype(F32)).astype(x.dtype)


def conformer_conv_branch(h, buf, w_in, w_dw, b_dw, ln_g, ln_b, w_out):
    a, gb, z = jnp.split(h @ w_in, 3, axis=-1)
    glu = a * jax.nn.sigmoid(gb)
    seq_in = jnp.concatenate([buf.astype(glu.dtype), glu], axis=1)
    e = w_dw.shape[-1]
    conv = lax.conv_general_dilated(
        seq_in, w_dw[:, None, :].astype(seq_in.dtype), window_strides=(1,), padding='VALID',
        dimension_numbers=('NWC', 'WIO', 'NWC'), feature_group_count=e) + b_dw
    y = jax.nn.silu(layer_norm(conv, ln_g, ln_b)) * jax.nn.silu(z)
    return y @ w_out, seq_in[:, -(CONV_W - 1):]


def moba_attention(q, k, v, q_offset):
    B, Lq, H, hd = q.shape
    qc_len = min(Q_CHUNK, Lq)
    n_chunks = -(-Lq // qc_len)
    lq_pad = n_chunks * qc_len
    lk = k.shape[1]
    lk_pad = -(-(q_offset + lq_pad) // MOBA_BLOCK) * MOBA_BLOCK
    n_blocks = lk_pad // MOBA_BLOCK
    ks = min(MOBA_TOPK, n_blocks)
    n_sel = ks * MOBA_BLOCK
    scale = hd ** -0.5
    q_p = jnp.pad(q, ((0, 0), (0, lq_pad - Lq), (0, 0), (0, 0)))
    kpad = ((0, 0), (0, lk_pad - lk), (0, 0), (0, 0))
    k_p = jnp.pad(k, kpad)
    v_p = jnp.pad(v, kpad)
    kb = k_p.reshape(B, n_blocks, MOBA_BLOCK, H, hd)
    vb = v_p.reshape(B, n_blocks, MOBA_BLOCK, H, hd)
    k_mean = jnp.mean(kb.astype(F32), axis=2)
    b_idx = jnp.arange(B)[:, None, None, None]
    h_idx = jnp.arange(H)[None, None, :, None]
    blk_ids = jnp.arange(n_blocks)

    def chunk(c):
        start = q_offset + c * qc_len
        blk = start // MOBA_BLOCK
        qc = lax.dynamic_slice_in_dim(q_p, c * qc_len, qc_len, axis=1)
        gate = jnp.einsum('bqhd,bnhd->bqhn', qc.astype(F32), k_mean)
        gate = jnp.where(blk_ids < blk, gate, -jnp.inf)
        _, sel = lax.top_k(gate, ks)
        sel_valid = sel < blk
        k_sel = kb[b_idx, sel, :, h_idx]
        v_sel = vb[b_idx, sel, :, h_idx]
        s_sel = jnp.einsum('bqhd,bqhskd->bqhsk', qc, k_sel).astype(F32) * scale
        s_sel = jnp.where(sel_valid[..., None], s_sel, -jnp.inf)
        k_own = lax.dynamic_slice_in_dim(k_p, blk * MOBA_BLOCK, MOBA_BLOCK, axis=1)
        v_own = lax.dynamic_slice_in_dim(v_p, blk * MOBA_BLOCK, MOBA_BLOCK, axis=1)
        s_own = jnp.einsum('bqhd,bkhd->bqhk', qc, k_own).astype(F32) * scale
        q_pos = start + jnp.arange(qc_len)
        k_pos = blk * MOBA_BLOCK + jnp.arange(MOBA_BLOCK)
        causal = (k_pos[None, :] <= q_pos[:, None])[None, :, None, :]
        s_own = jnp.where(causal, s_own, -jnp.inf)
        logits = jnp.concatenate([s_sel.reshape(B, qc_len, H, n_sel), s_own], axis=-1)
        p = jax.nn.softmax(logits, axis=-1).astype(v.dtype)
        p_sel = p[..., :n_sel].reshape(B, qc_len, H, ks, MOBA_BLOCK)
        return (jnp.einsum('bqhsk,bqhskd->bqhd', p_sel, v_sel)
                + jnp.einsum('bqhk,bkhd->bqhd', p[..., n_sel:], v_own))

    out = lax.map(chunk, jnp.arange(n_chunks))
    out = jnp.moveaxis(out, 0, 1).reshape(B, lq_pad, H, hd)
    return out[:, :Lq]


def moba_branch(h, k, v, q_offset, w_in, w_out):
    B, L, _ = h.shape
    q, z = jnp.split(h @ w_in, 2, axis=-1)
    o = moba_attention(q.reshape(B, L, N_HEADS, HEAD_DIM), k, v, q_offset)
    return (o.reshape(B, L, ATT_W) * jax.nn.silu(z)) @ w_out


def trunk(x, conv_bufs, past_k, past_v, g_a, w_in_a, w_dw, b_dw, ln_g_a, ln_b_a, w_out_a,
          g_kv, w_k, w_v, g_b, w_in_b, w_out_b, g_final):
    B, L, _ = x.shape
    q_offset = past_k.shape[1]
    new_bufs = []
    k_new = v_new = k_all = v_all = None
    for layer in range(DEPTH):
        if layer < N_A_LAYERS:
            i = layer
            y, buf = conformer_conv_branch(rms_norm(x, g_a[i]), conv_bufs[i], w_in_a[i], w_dw[i],
                                           b_dw[i], ln_g_a[i], ln_b_a[i], w_out_a[i])
            x = x + y
            new_bufs.append(buf)
            if layer == N_A_LAYERS - 1:
                hk = rms_norm(x, g_kv)
                k_new = (hk @ w_k).reshape(B, L, N_HEADS, HEAD_DIM)
                v_new = (hk @ w_v).reshape(B, L, N_HEADS, HEAD_DIM)
                k_all = jnp.concatenate([past_k.astype(k_new.dtype), k_new], axis=1)
                v_all = jnp.concatenate([past_v.astype(v_new.dtype), v_new], axis=1)
        else:
            j = layer - N_A_LAYERS
            x = x + moba_branch(rms_norm(x, g_b[j]), k_all, v_all, q_offset, w_in_b[j], w_out_b[j])
    return rms_norm(x, g_final), k_new, v_new, jnp.stack(new_bufs)


def setup_inputs(seed: int = 0) -> dict:
    key = jax.random.key(seed)
    ks = jax.random.split(key, 24)
    n_pages = PAST_LEN // PAGE_SIZE
    used = DEC_BATCH * n_pages
    n_pool = used + max(1, used // 4)
    nrm = lambda k, shape, s: jax.random.normal(k, shape, F32) * s
    page_table = jax.random.permutation(ks[0], n_pool)[:used].reshape(DEC_BATCH, n_pages).astype(jnp.int32)
    return {
        'x_prompt': nrm(ks[1], (BATCH, SEQ, D_MODEL), 1.0),
        'x_sample': nrm(ks[2], (DEC_BATCH, DEC_SEQ, D_MODEL), 1.0),
        'state_conv': nrm(ks[3], (N_A_LAYERS, DEC_BATCH, CONV_W - 1, E_A), 0.5),
        'cache_k': nrm(ks[4], (n_pool, PAGE_SIZE, N_HEADS, HEAD_DIM), 1.0),
        'cache_v': nrm(ks[5], (n_pool, PAGE_SIZE, N_HEADS, HEAD_DIM), 1.0),
        'page_table': page_table,
        'g_a': 1.0 + nrm(ks[6], (N_A_LAYERS, D_MODEL), 0.01),
        'w_in_a': nrm(ks[7], (N_A_LAYERS, D_MODEL, 3 * E_A), D_MODEL ** -0.5),
        'w_dw': nrm(ks[8], (N_A_LAYERS, CONV_W, E_A), CONV_W ** -0.5),
        'b_dw': nrm(ks[9], (N_A_LAYERS, E_A), 0.01),
        'ln_g_a': 1.0 + nrm(ks[10], (N_A_LAYERS, E_A), 0.01),
        'ln_b_a': nrm(ks[11], (N_A_LAYERS, E_A), 0.01),
        'w_out_a': nrm(ks[12], (N_A_LAYERS, E_A, D_MODEL), E_A ** -0.5),
        'g_kv': 1.0 + nrm(ks[13], (D_MODEL,), 0.01),
        'w_k': nrm(ks[14], (D_MODEL, ATT_W), D_MODEL ** -0.5),
        'w_v': nrm(ks[15], (D_MODEL, ATT_W), D_MODEL ** -0.5),
        'g_b': 1.0 + nrm(ks[16], (N_B_LAYERS, D_MODEL), 0.01),
        'w_in_b': nrm(ks[17], (N_B_LAYERS, D_MODEL, 2 * ATT_W), D_MODEL ** -0.5),
        'w_out_b': nrm(ks[18], (N_B_LAYERS, ATT_W, D_MODEL), ATT_W ** -0.5),
        'g_final': 1.0 + nrm(ks[19], (D_MODEL,), 0.01),
    }


def reference(x_prompt, x_sample, state_conv, cache_k, cache_v, page_table,
              g_a, w_in_a, w_dw, b_dw, ln_g_a, ln_b_a, w_out_a,
              g_kv, w_k, w_v, g_b, w_in_b, w_out_b, g_final):
    weights = (g_a, w_in_a, w_dw, b_dw, ln_g_a, ln_b_a, w_out_a, g_kv, w_k, w_v, g_b, w_in_b, w_out_b, g_final)
    bp = x_prompt.shape[0]
    zero_bufs = jnp.zeros((N_A_LAYERS, bp, CONV_W - 1, E_A), x_prompt.dtype)
    empty_kv = jnp.zeros((bp, 0, N_HEADS, HEAD_DIM), x_prompt.dtype)
    y_prompt, k_prompt, v_prompt, conv_prompt = trunk(x_prompt, zero_bufs, empty_kv, empty_kv, *weights)
    bs = x_sample.shape[0]
    n_pages = page_table.shape[1]
    past_k = cache_k[page_table].reshape(bs, n_pages * cache_k.shape[1], N_HEADS, HEAD_DIM)
    past_v = cache_v[page_table].reshape(bs, n_pages * cache_v.shape[1], N_HEADS, HEAD_DIM)
    y_sample, k_sample, v_sample, conv_sample = trunk(x_sample, state_conv, past_k, past_v, *weights)
    return (y_prompt, y_sample, k_prompt, v_prompt, conv_prompt, k_sample, v_sample, conv_sample)
```

```python
import functools

import jax
import jax.numpy as jnp
from jax import lax
from jax.experimental import pallas as pl
from jax.experimental.pallas import tpu as pltpu

F32 = jnp.float32
BF16 = jnp.bfloat16

EPS = 1e-6
CONV_W = 31
HALO = 32
HEAD_DIM = 64
MOBA_BLOCK = 256
MOBA_TOPK = 3
LANES = 128
SUBLANES = 8
VMEM_LIMIT = 56 * 1024 * 1024
NEG_INF = float("-inf")

_NT = (((1,), (1,)), ((), ()))


def _cparams(*sem):
    return pltpu.CompilerParams(dimension_semantics=sem, vmem_limit_bytes=VMEM_LIMIT)


def _rms(x, g):
    return x * lax.rsqrt(jnp.mean(x * x, axis=-1, keepdims=True) + EPS) * g


def _silu(x):
    return x * jax.nn.sigmoid(x)


def _full(shape):
    return pl.BlockSpec(shape, lambda *_: (0,) * len(shape))


def _conv_tap_groups():
    groups = {}
    for j in range(CONV_W):
        off = HALO - (CONV_W - 1) + j
        groups.setdefault(off % SUBLANES, []).append((j, off - off % SUBLANES))
    return groups


def _a_layer_kernel(x_ref, g_ref, win_ref, wdw_ref, bdw_ref, lng_ref, lnb_ref, wout_ref,
                    o_ref, buf_ref, s_ref, sz_ref, xn_ref, y_ref, *, tm, e, rc, ch):
    t = pl.program_id(1)

    @pl.when(t == 0)
    def _():
        s_ref[0:HALO, :] = jnp.zeros((HALO, e), F32)

    xn_ref[...] = _rms(x_ref[...], g_ref[...]).astype(BF16)
    for c in range(e // ch):
        xn = xn_ref[...]
        a = jnp.dot(xn, win_ref[:, c * ch:(c + 1) * ch], preferred_element_type=F32)
        gb = jnp.dot(xn, win_ref[:, e + c * ch:e + (c + 1) * ch], preferred_element_type=F32)
        z = jnp.dot(xn, win_ref[:, 2 * e + c * ch:2 * e + (c + 1) * ch], preferred_element_type=F32)
        s_ref[HALO:HALO + tm, c * ch:(c + 1) * ch] = a * jax.nn.sigmoid(gb)
        sz_ref[:, c * ch:(c + 1) * ch] = _silu(z)

    @pl.when(t == pl.num_programs(1) - 1)
    def _():
        buf_ref[...] = s_ref[tm:tm + HALO, :]

    groups = _conv_tap_groups()
    ncb = e // LANES

    def row_chunk(ci, _):
        r0 = pl.multiple_of(ci * rc, rc)

        def conv_cb(cb, s1):
            c0 = pl.multiple_of(cb * LANES, LANES)
            w = s_ref[pl.ds(r0, rc + HALO), pl.ds(c0, LANES)]
            acc = jnp.broadcast_to(bdw_ref[:, pl.ds(c0, LANES)], (rc, LANES))
            for sh, taps in sorted(groups.items()):
                ws = w if sh == 0 else w[sh:sh + rc + HALO - SUBLANES]
                for j, a0 in taps:
                    acc = acc + wdw_ref[pl.ds(j, 1), pl.ds(c0, LANES)] * ws[a0:a0 + rc]
            s_ref[pl.ds(r0, rc), pl.ds(c0, LANES)] = acc
            return s1 + acc

        s1 = lax.fori_loop(0, ncb, conv_cb, jnp.zeros((rc, LANES), F32))
        mu = jnp.sum(s1, axis=-1, keepdims=True) * (1.0 / e)

        def var_cb(cb, s2):
            c0 = pl.multiple_of(cb * LANES, LANES)
            d = s_ref[pl.ds(r0, rc), pl.ds(c0, LANES)] - mu
            return s2 + d * d

        s2 = lax.fori_loop(0, ncb, var_cb, jnp.zeros((rc, LANES), F32))
        rstd = lax.rsqrt(jnp.sum(s2, axis=-1, keepdims=True) * (1.0 / e) + EPS)

        def norm_cb(cb, _):
            c0 = pl.multiple_of(cb * LANES, LANES)
            cv = s_ref[pl.ds(r0, rc), pl.ds(c0, LANES)]
            yn = (cv - mu) * rstd * lng_ref[:, pl.ds(c0, LANES)] + lnb_ref[:, pl.ds(c0, LANES)]
            y = _silu(yn) * sz_ref[pl.ds(r0, rc), pl.ds(c0, LANES)]
            y_ref[pl.ds(r0, rc), pl.ds(c0, LANES)] = y.astype(BF16)
            return 0

        lax.fori_loop(0, ncb, norm_cb, 0)
        return 0

    lax.fori_loop(0, tm // rc, row_chunk, 0)
    s_ref[0:HALO, :] = s_ref[tm:tm + HALO, :]
    o_ref[...] = x_ref[...] + jnp.dot(y_ref[...], wout_ref[...], preferred_element_type=F32)


def _a_layer_prompt(x, g, w_in, w_dw, b_dw, ln_g, ln_b, w_out, *, tm=256, rc=64, ch=512):
    b, l, d = x.shape
    e = w_dw.shape[-1]
    assert l % tm == 0 and tm % rc == 0 and tm >= HALO and e % ch == 0
    kern = functools.partial(_a_layer_kernel, tm=tm, e=e, rc=rc, ch=ch)
    row = lambda w: w.reshape(1, -1)
    return pl.pallas_call(
        kern,
        grid=(b, l // tm),
        in_specs=[
            pl.BlockSpec((None, tm, d), lambda i, t: (i, t, 0)),
            _full((1, d)), _full((d, 3 * e)), _full((CONV_W, e)),
            _full((1, e)), _full((1, e)), _full((1, e)), _full((e, d)),
        ],
        out_specs=[
            pl.BlockSpec((None, tm, d), lambda i, t: (i, t, 0)),
            pl.BlockSpec((None, HALO, e), lambda i, t: (i, 0, 0)),
        ],
        out_shape=[jax.ShapeDtypeStruct((b, l, d), F32), jax.ShapeDtypeStruct((b, HALO, e), F32)],
        scratch_shapes=[
            pltpu.VMEM((HALO + tm, e), F32), pltpu.VMEM((tm, e), F32),
            pltpu.VMEM((tm, d), BF16), pltpu.VMEM((tm, e), BF16),
        ],
        compiler_params=_cparams("arbitrary", "arbitrary"),
        name="a_layer_prompt",
    )(x, row(g), w_in, w_dw, row(b_dw), row(ln_g), row(ln_b), w_out)


def _kv_kernel(*refs, extras):
    if extras:
        x_ref, g_ref, wk_ref, wv_ref, wvt_ref, k_ref, v_ref, kb_ref, vt_ref, km_ref = refs
    else:
        x_ref, g_ref, wk_ref, wv_ref, k_ref, v_ref = refs
    hk = _rms(x_ref[...], g_ref[...]).astype(BF16)
    k = jnp.dot(hk, wk_ref[...], preferred_element_type=F32)
    k_ref[...] = k
    v_ref[...] = jnp.dot(hk, wv_ref[...], preferred_element_type=F32)
    if extras:
        kb_ref[...] = k.astype(BF16)
        vt_ref[...] = lax.dot_general(wvt_ref[...], hk, _NT, preferred_element_type=F32).astype(BF16)
        km_ref[...] = jnp.sum(k, axis=0, keepdims=True) * (1.0 / MOBA_BLOCK)


def _kv_proj(x, g, w_k, w_v, w_vt=None):
    b, l, d = x.shape
    a = w_k.shape[1]
    extras = w_vt is not None
    tm = MOBA_BLOCK if extras else l
    assert l % tm == 0
    xspec = pl.BlockSpec((None, tm, d), lambda i, t: (i, t, 0))
    ospec = pl.BlockSpec((None, tm, a), lambda i, t: (i, t, 0))
    in_specs = [xspec, _full((1, d)), _full((d, a)), _full((d, a))]
    out_specs = [ospec, ospec]
    out_shape = [jax.ShapeDtypeStruct((b, l, a), F32)] * 2
    args = [x, g.reshape(1, d), w_k, w_v]
    if extras:
        in_specs.append(_full((a, d)))
        args.append(w_vt)
        out_specs += [ospec, pl.BlockSpec((None, a, tm), lambda i, t: (i, 0, t)),
                      pl.BlockSpec((None, None, 1, a), lambda i, t: (i, t, 0, 0))]
        out_shape += [jax.ShapeDtypeStruct((b, l, a), BF16), jax.ShapeDtypeStruct((b, a, l), BF16),
                      jax.ShapeDtypeStruct((b, l // tm, 1, a), F32)]
    return pl.pallas_call(
        functools.partial(_kv_kernel, extras=extras),
        grid=(b, l // tm), in_specs=in_specs, out_specs=out_specs, out_shape=out_shape,
        compiler_params=_cparams("parallel", "parallel"), name="kv_proj",
    )(*args)


def _qz_kernel(x_ref, g_ref, w_ref, q_ref, qs_ref, sz_ref, *, a):
    xn = _rms(x_ref[...], g_ref[...]).astype(BF16)
    q = jnp.dot(xn, w_ref[:, 0:a], preferred_element_type=F32)
    z = jnp.dot(xn, w_ref[:, a:2 * a], preferred_element_type=F32)
    q_ref[...] = q
    qs_ref[...] = (q * (HEAD_DIM ** -0.5)).astype(BF16)
    sz_ref[...] = _silu(z)


def _qz_proj(x, g, w_in, *, tm):
    b, l, d = x.shape
    a = w_in.shape[1] // 2
    assert l % tm == 0
    xspec = pl.BlockSpec((None, tm, d), lambda i, t: (i, t, 0))
    ospec = pl.BlockSpec((None, tm, a), lambda i, t: (i, t, 0))
    return pl.pallas_call(
        functools.partial(_qz_kernel, a=a),
        grid=(b, l // tm),
        in_specs=[xspec, _full((1, d)), _full((d, 2 * a))],
        out_specs=[ospec, ospec, ospec],
        out_shape=[jax.ShapeDtypeStruct((b, l, a), F32), jax.ShapeDtypeStruct((b, l, a), BF16),
                   jax.ShapeDtypeStruct((b, l, a), F32)],
        compiler_params=_cparams("parallel", "parallel"), name="qz_proj",
    )(x, g.reshape(1, d), w_in)


def _out_kernel(o_ref, sz_ref, x_ref, w_ref, gf_ref, y_ref, *, final):
    y = (o_ref[...] * sz_ref[...]).astype(BF16)
    xo = x_ref[...] + jnp.dot(y, w_ref[...], preferred_element_type=F32)
    y_ref[...] = _rms(xo, gf_ref[...]) if final else xo


def _out_proj(o, sz, x, w_out, g_final, *, final, tm):
    b, l, d = x.shape
    a = o.shape[-1]
    assert l % tm == 0
    aspec = pl.BlockSpec((None, tm, a), lambda i, t: (i, t, 0))
    xspec = pl.BlockSpec((None, tm, d), lambda i, t: (i, t, 0))
    return pl.pallas_call(
        functools.partial(_out_kernel, final=final),
        grid=(b, l // tm),
        in_specs=[aspec, aspec, xspec, _full((a, d)), _full((1, d))],
        out_specs=xspec,
        out_shape=jax.ShapeDtypeStruct((b, l, d), F32),
        compiler_params=_cparams("parallel", "parallel"), name="out_proj",
    )(o, sz, x, w_out, g_final.reshape(1, d))


def _moba_prompt_kernel(q_ref, qs_ref, k_ref, vt_ref, km_ref, o_ref, sel_ref, qh_ref, acc_ref, *, nblk):
    i = pl.program_id(2)
    blk = MOBA_BLOCK
    lane = lax.broadcasted_iota(jnp.int32, (1, LANES), 1)
    n_iota = lax.broadcasted_iota(jnp.int32, (nblk, blk), 0)
    q32 = q_ref[...]
    qs = qs_ref[...]
    km = km_ref[...]
    for hh in range(2):
        hmask = (lane >> 6) == hh
        g = lax.dot_general(km, jnp.where(hmask, q32, 0.0), _NT,
                            precision=lax.Precision.HIGHEST, preferred_element_type=F32)
        g = jnp.where(n_iota < i, g, NEG_INF)
        sel = jnp.zeros((nblk, blk), F32)
        for _ in range(MOBA_TOPK):
            m = jnp.max(g, axis=0, keepdims=True)
            idx = jnp.min(jnp.where(g == m, n_iota, nblk), axis=0, keepdims=True)
            hit = n_iota == idx
            sel = jnp.where(hit & (m > NEG_INF), 1.0, sel)
            g = jnp.where(hit, NEG_INF, g)
        sel_ref[hh] = sel
        qh_ref[hh] = jnp.where(hmask, qs, jnp.zeros_like(qs))

    k_iota = lax.broadcasted_iota(jnp.int32, (blk, blk), 0)
    q_iota = lax.broadcasted_iota(jnp.int32, (blk, blk), 1)
    own = pl.multiple_of(i * blk, blk)
    kb = k_ref[pl.ds(own, blk), :]
    stats = []
    for hh in range(2):
        st = lax.dot_general(kb, qh_ref[hh], _NT, preferred_element_type=F32)
        st = jnp.where(k_iota <= q_iota, st, NEG_INF)
        m = jnp.max(st, axis=0, keepdims=True)
        p = jnp.exp(st - m)
        stats += [m, jnp.sum(p, axis=0, keepdims=True)]
        vt = vt_ref[hh * HEAD_DIM:(hh + 1) * HEAD_DIM, pl.ds(own, blk)]
        acc_ref[hh] = jnp.dot(vt, p.astype(BF16), preferred_element_type=F32)

    def body(n, carry):
        start = pl.multiple_of(n * blk, blk)
        kb = k_ref[pl.ds(start, blk), :]
        out = []
        for hh in range(2):
            m, l = carry[2 * hh], carry[2 * hh + 1]
            st = lax.dot_general(kb, qh_ref[hh], _NT, preferred_element_type=F32)
            st = jnp.where(sel_ref[hh, pl.ds(n, 1), :] > 0.0, st, NEG_INF)
            m_new = jnp.maximum(m, jnp.max(st, axis=0, keepdims=True))
            alpha = jnp.exp(m - m_new)
            p = jnp.exp(st - m_new)
            vt = vt_ref[hh * HEAD_DIM:(hh + 1) * HEAD_DIM, pl.ds(start, blk)]
            acc_ref[hh] = alpha * acc_ref[hh] + jnp.dot(vt, p.astype(BF16), preferred_element_type=F32)
            out += [m_new, alpha * l + jnp.sum(p, axis=0, keepdims=True)]
        return tuple(out)

    stats = lax.fori_loop(0, i, body, tuple(stats))
    ot = jnp.concatenate([acc_ref[0] / stats[1], acc_ref[1] / stats[3]], axis=0)
    o_ref[...] = ot.T


def _moba_prompt(q, qs, kb, vt, km):
    b, l, a = q.shape
    blk = MOBA_BLOCK
    nblk = l // blk
    assert l % blk == 0 and a % LANES == 0 and LANES == 2 * HEAD_DIM
    qspec = pl.BlockSpec((None, blk, LANES), lambda i, h, t: (i, t, h))
    return pl.pallas_call(
        functools.partial(_moba_prompt_kernel, nblk=nblk),
        grid=(b, a // LANES, nblk),
        in_specs=[
            qspec, qspec,
            pl.BlockSpec((None, l, LANES), lambda i, h, t: (i, 0, h)),
            pl.BlockSpec((None, LANES, l), lambda i, h, t: (i, h, 0)),
            pl.BlockSpec((None, nblk, LANES), lambda i, h, t: (i, 0, h)),
        ],
        out_specs=qspec,
        out_shape=jax.ShapeDtypeStruct((b, l, a), F32),
        scratch_shapes=[
            pltpu.VMEM((2, nblk, blk), F32), pltpu.VMEM((2, blk, LANES), BF16),
            pltpu.VMEM((2, HEAD_DIM, blk), F32),
        ],
        compiler_params=_cparams("parallel", "parallel", "arbitrary"),
        name="moba_prompt",
    )(q, qs, kb, vt, km)


def _in_glu_kernel(x_ref, g_ref, wa_ref, wg_ref, wz_ref, glu_ref, sz_ref):
    xn = _rms(x_ref[...], g_ref[...]).astype(BF16)
    a = jnp.dot(xn, wa_ref[...], preferred_element_type=F32)
    gb = jnp.dot(xn, wg_ref[...], preferred_element_type=F32)
    z = jnp.dot(xn, wz_ref[...], preferred_element_type=F32)
    glu_ref[...] = a * jax.nn.sigmoid(gb)
    sz_ref[...] = _silu(z)


def _in_glu(x, g, w_in, *, ch=512):
    n, d = x.shape
    e = w_in.shape[1] // 3
    nc = e // ch
    wspec = lambda k: pl.BlockSpec((d, ch), lambda c: (0, c + k * nc))
    ospec = pl.BlockSpec((n, ch), lambda c: (0, c))
    return pl.pallas_call(
        _in_glu_kernel,
        grid=(nc,),
        in_specs=[_full((n, d)), _full((1, d)), wspec(0), wspec(1), wspec(2)],
        out_specs=[ospec, ospec],
        out_shape=[jax.ShapeDtypeStruct((n, e), F32)] * 2,
        compiler_params=_cparams("parallel"), name="in_glu_sample",
    )(x, g.reshape(1, d), w_in, w_in, w_in)


def _dec_conv_kernel(st_ref, glu_ref, sz_ref, wdw_ref, bdw_ref, lng_ref, lnb_ref,
                     y_ref, ns_ref, conv_ref, *, bt):
    nh = CONV_W - 1
    w_hist = wdw_ref[0:nh, :]
    for i in range(bt):
        conv_ref[i:i + 1, :] = jnp.sum(st_ref[i] * w_hist, axis=0, keepdims=True)
        ns_ref[i, 0:nh - 1, :] = st_ref[i, 1:nh, :]
        ns_ref[i, nh - 1:nh, :] = glu_ref[i:i + 1, :]
    conv = conv_ref[...] + glu_ref[...] * wdw_ref[nh:nh + 1, :] + bdw_ref[...]
    mu = jnp.mean(conv, axis=-1, keepdims=True)
    d = conv - mu
    yn = d * lax.rsqrt(jnp.mean(d * d, axis=-1, keepdims=True) + EPS) * lng_ref[...] + lnb_ref[...]
    y_ref[...] = (_silu(yn) * sz_ref[...]).astype(BF16)


def _dec_conv(state, glu, sz, w_dw, b_dw, ln_g, ln_b, *, bt=16):
    n, nh, e = state.shape
    assert n % bt == 0 and nh == CONV_W - 1
    rspec = pl.BlockSpec((bt, e), lambda i: (i, 0))
    sspec = pl.BlockSpec((bt, nh, e), lambda i: (i, 0, 0))
    row = lambda w: w.reshape(1, e)
    return pl.pallas_call(
        functools.partial(_dec_conv_kernel, bt=bt),
        grid=(n // bt,),
        in_specs=[sspec, rspec, rspec, _full((CONV_W, e)), _full((1, e)), _full((1, e)), _full((1, e))],
        out_specs=[rspec, sspec],
        out_shape=[jax.ShapeDtypeStruct((n, e), BF16), jax.ShapeDtypeStruct((n, nh, e), F32)],
        scratch_shapes=[pltpu.VMEM((bt, e), F32)],
        compiler_params=_cparams("parallel"), name="conv_sample",
    )(state, glu, sz, w_dw, row(b_dw), row(ln_g), row(ln_b))


def _res_proj_kernel(y_ref, x_ref, w_ref, o_ref):
    o_ref[...] = x_ref[...] + jnp.dot(y_ref[...], w_ref[...], preferred_element_type=F32)


def _res_proj(y, x, w):
    n, d = x.shape
    e = y.shape[1]
    return pl.pallas_call(
        _res_proj_kernel, grid=(1,),
        in_specs=[_full((n, e)), _full((n, d)), _full((e, d))],
        out_specs=_full((n, d)), out_shape=jax.ShapeDtypeStruct((n, d), F32),
        compiler_params=_cparams("arbitrary"), name="res_proj_sample",
    )(y, x, w)


def _head_rows(shape):
    r = lax.broadcasted_iota(jnp.int32, shape, 0)
    c = lax.broadcasted_iota(jnp.int32, shape, 1)
    return r == (c >> 6)


def _dec_scores_kernel(pt_ref, q_ref, *refs, pg, nh):
    del pt_ref
    k_refs, (sc_ref, km_ref) = refs[:pg], refs[pg:]
    page = k_refs[0].shape[0]
    a = q_ref.shape[-1]
    qrows = jnp.where(_head_rows((LANES, a)), q_ref[...], 0.0).astype(BF16)
    ppb = MOBA_BLOCK // page
    ksum = None
    for p in range(pg):
        kp = k_refs[p][...]
        s = lax.dot_general(kp.astype(BF16), qrows, _NT, preferred_element_type=F32)
        sc_ref[p] = s.T[0:nh, :] * (HEAD_DIM ** -0.5)
        part = jnp.sum(kp, axis=0, keepdims=True)
        ksum = part if p % ppb == 0 else ksum + part
        if p % ppb == ppb - 1:
            km_ref[p // ppb:p // ppb + 1, :] = ksum * (1.0 / MOBA_BLOCK)


def _dec_scores(page_table, q, cache_k, *, pg=8):
    bd, npg = page_table.shape
    _, page, a = cache_k.shape
    nh = a // HEAD_DIM
    assert npg % pg == 0 and MOBA_BLOCK % page == 0 and pg % (MOBA_BLOCK // page) == 0 and page == LANES
    bps = pg * page // MOBA_BLOCK
    kspec = lambda p: pl.BlockSpec((None, page, a), lambda b, j, pt: (pt[b, j * pg + p], 0, 0))
    grid_spec = pltpu.PrefetchScalarGridSpec(
        num_scalar_prefetch=1, grid=(bd, npg // pg),
        in_specs=[pl.BlockSpec((None, 1, a), lambda b, j, pt: (b, 0, 0))] + [kspec(p) for p in range(pg)],
        out_specs=[pl.BlockSpec((None, pg, nh, page), lambda b, j, pt: (b, j, 0, 0)),
                   pl.BlockSpec((None, None, bps, a), lambda b, j, pt: (b, j, 0, 0))],
    )
    return pl.pallas_call(
        functools.partial(_dec_scores_kernel, pg=pg, nh=nh),
        grid_spec=grid_spec,
        out_shape=[jax.ShapeDtypeStruct((bd, npg, nh, page), F32),
                   jax.ShapeDtypeStruct((bd, npg // pg, bps, a), F32)],
        compiler_params=_cparams("parallel", "arbitrary"), name="scores_sample",
    )(page_table, q, *([cache_k] * pg))


def _dec_pv_kernel(pt_ref, sc_ref, km_ref, q_ref, kn_ref, vn_ref, *refs, pg, nh, nblk):
    del pt_ref
    v_refs, (o_ref, p_ref, pown_ref, acc_ref) = refs[:pg], refs[pg:]
    j = pl.program_id(1)
    page = v_refs[0].shape[0]
    a = q_ref.shape[-1]
    ppb = MOBA_BLOCK // page
    hrows = _head_rows((nh, a))

    @pl.when(j == 0)
    def _():
        q = q_ref[...]
        g = [jnp.sum(jnp.where(hrows, km_ref[n:n + 1, :] * q, 0.0), axis=-1, keepdims=True)
             for n in range(nblk)]
        sel = [jnp.zeros((nh, 1), F32) for _ in range(nblk)]
        for _ in range(min(MOBA_TOPK, nblk)):
            m = functools.reduce(jnp.maximum, g)
            idx = functools.reduce(jnp.minimum, [jnp.where(g[n] == m, n, nblk) for n in range(nblk)])
            for n in range(nblk):
                hit = idx == n
                sel[n] = jnp.where(hit, 1.0, sel[n])
                g[n] = jnp.where(hit, NEG_INF, g[n])
        s_own = jnp.sum(jnp.where(hrows, kn_ref[...] * q, 0.0), axis=-1, keepdims=True) * (HEAD_DIM ** -0.5)
        m = s_own
        sc = []
        for pgi in range(nblk * ppb):
            s = jnp.where(sel[pgi // ppb] > 0.0, sc_ref[pgi], NEG_INF)
            sc.append(s)
            m = jnp.maximum(m, jnp.max(s, axis=-1, keepdims=True))
        p_own = jnp.exp(s_own - m)
        l = p_own
        for pgi in range(nblk * ppb):
            p = jnp.exp(sc[pgi] - m)
            sc[pgi] = p
            l = l + jnp.sum(p, axis=-1, keepdims=True)
        inv = 1.0 / l
        for pgi in range(nblk * ppb):
            p_ref[pgi] = sc[pgi] * inv
        pown_ref[...] = jnp.broadcast_to(p_own * inv, pown_ref.shape)
        acc_ref[...] = jnp.zeros_like(acc_ref)

    acc = acc_ref[...]
    for p in range(pg):
        acc = acc + jnp.dot(p_ref[j * pg + p].astype(BF16), v_refs[p][...].astype(BF16),
                            preferred_element_type=F32)
    acc_ref[...] = acc

    @pl.when(j == pl.num_programs(1) - 1)
    def _():
        full = acc_ref[...] + pown_ref[:, 0:1] * vn_ref[...]
        o_ref[...] = jnp.sum(jnp.where(hrows, full, 0.0), axis=0, keepdims=True)


def _dec_pv(page_table, scores, kmean, q, k_new, v_new, cache_v, *, pg=8):
    bd, npg = page_table.shape
    _, page, a = cache_v.shape
    nh = a // HEAD_DIM
    nblk = kmean.shape[1]
    assert npg % pg == 0 and nblk * MOBA_BLOCK == npg * page
    vspec = lambda p: pl.BlockSpec((None, page, a), lambda b, j, pt: (pt[b, j * pg + p], 0, 0))
    rowspec = pl.BlockSpec((None, 1, a), lambda b, j, pt: (b, 0, 0))
    grid_spec = pltpu.PrefetchScalarGridSpec(
        num_scalar_prefetch=1, grid=(bd, npg // pg),
        in_specs=[pl.BlockSpec((None, npg, nh, page), lambda b, j, pt: (b, 0, 0, 0)),
                  pl.BlockSpec((None, nblk, a), lambda b, j, pt: (b, 0, 0)),
                  rowspec, rowspec, rowspec] + [vspec(p) for p in range(pg)],
        out_specs=rowspec,
        scratch_shapes=[pltpu.VMEM((npg, nh, page), F32), pltpu.VMEM((nh, LANES), F32),
                        pltpu.VMEM((nh, a), F32)],
    )
    return pl.pallas_call(
        functools.partial(_dec_pv_kernel, pg=pg, nh=nh, nblk=nblk),
        grid_spec=grid_spec,
        out_shape=jax.ShapeDtypeStruct((bd, 1, a), F32),
        compiler_params=_cparams("parallel", "arbitrary"), name="pv_sample",
    )(page_table, scores, kmean, q, k_new, v_new, *([cache_v] * pg))


def kernel(x_prompt, x_sample, state_conv, cache_k, cache_v, page_table, g_a, w_in_a, w_dw, b_dw, ln_g_a, ln_b_a, w_out_a, g_kv, w_k, w_v, g_b, w_in_b, w_out_b, g_final):
    n_a, n_b = w_in_a.shape[0], w_in_b.shape[0]
    bp, lp, d = x_prompt.shape
    bd, ld, _ = x_sample.shape
    n_pool, page, nh, hd = cache_k.shape
    a = nh * hd
    nhist = CONV_W - 1
    assert ld == 1 and hd == HEAD_DIM and lp >= nhist
    assert (page_table.shape[1] * page) % MOBA_BLOCK == 0

    w_in_a_h, w_out_a_h = w_in_a.astype(BF16), w_out_a.astype(BF16)
    w_k_h, w_v_h, w_vt_h = w_k.astype(BF16), w_v.astype(BF16), w_v.T.astype(BF16)
    w_in_b_h, w_out_b_h = w_in_b.astype(BF16), w_out_b.astype(BF16)

    x = x_prompt
    bufs = []
    for i in range(n_a):
        x, buf = _a_layer_prompt(x, g_a[i], w_in_a_h[i], w_dw[i], b_dw[i], ln_g_a[i], ln_b_a[i], w_out_a_h[i])
        bufs.append(buf[:, HALO - nhist:, :])
    k_p, v_p, kb, vt, km = _kv_proj(x, g_kv, w_k_h, w_v_h, w_vt_h)
    km = km.reshape(bp, lp // MOBA_BLOCK, a)
    for j in range(n_b):
        q, qs, sz = _qz_proj(x, g_b[j], w_in_b_h[j], tm=512)
        o = _moba_prompt(q, qs, kb, vt, km)
        x = _out_proj(o, sz, x, w_out_b_h[j], g_final, final=(j == n_b - 1), tm=512)
    y_prompt = x
    conv_prompt = jnp.stack(bufs)

    xs = x_sample.reshape(bd, d)
    new_states = []
    for i in range(n_a):
        glu, sz = _in_glu(xs, g_a[i], w_in_a_h[i])
        y, ns = _dec_conv(state_conv[i], glu, sz, w_dw[i], b_dw[i], ln_g_a[i], ln_b_a[i])
        xs = _res_proj(y, xs, w_out_a_h[i])
        new_states.append(ns)
    k_s, v_s = _kv_proj(xs.reshape(1, bd, d), g_kv, w_k_h, w_v_h)
    ck = cache_k.reshape(n_pool, page, a)
    cv = cache_v.reshape(n_pool, page, a)
    k_row, v_row = k_s.reshape(bd, 1, a), v_s.reshape(bd, 1, a)
    xs3 = xs.reshape(1, bd, d)
    for j in range(n_b):
        q, _, sz = _qz_proj(xs3, g_b[j], w_in_b_h[j], tm=bd)
        q_row = q.reshape(bd, 1, a)
        scores, kmean = _dec_scores(page_table, q_row, ck)
        kmean = kmean.reshape(bd, -1, a)
        o = _dec_pv(page_table, scores, kmean, q_row, k_row, v_row, cv)
        xs3 = _out_proj(o.reshape(1, bd, a), sz, xs3, w_out_b_h[j], g_final, final=(j == n_b - 1), tm=bd)
    y_sample = xs3.reshape(bd, 1, d)

    return (y_prompt, y_sample,
            k_p.reshape(bp, lp, nh, hd), v_p.reshape(bp, lp, nh, hd), conv_prompt,
            k_s.reshape(bd, 1, nh, hd), v_s.reshape(bd, 1, nh, hd), jnp.stack(new_states))
```

```python
import functools

import jax
import jax.numpy as jnp
from jax import lax
from jax.experimental import pallas as pl
from jax.experimental.pallas import tpu as pltpu

F32 = jnp.float32
BF16 = jnp.bfloat16

EPS = 1e-6
CONV_W = 31
HALO = 32
HEAD_DIM = 64
MOBA_BLOCK = 256
MOBA_TOPK = 3
LANES = 128
SUBLANES = 8
VMEM_LIMIT = 56 * 1024 * 1024
NEG_INF = float("-inf")

_NT = (((1,), (1,)), ((), ()))


def _cparams(*sem):
    return pltpu.CompilerParams(dimension_semantics=sem, vmem_limit_bytes=VMEM_LIMIT)


def _rms(x, g):
    return x * lax.rsqrt(jnp.mean(x * x, axis=-1, keepdims=True) + EPS) * g


def _silu(x):
    return x * jax.nn.sigmoid(x)


def _full(shape):
    return pl.BlockSpec(shape, lambda *_: (0,) * len(shape))


def _conv_tap_groups():
    groups = {}
    for j in range(CONV_W):
        off = HALO - (CONV_W - 1) + j
        groups.setdefault(off % SUBLANES, []).append((j, off - off % SUBLANES))
    return groups


def _a_layer_kernel(x_ref, g_ref, win_ref, wdw_ref, bdw_ref, lng_ref, lnb_ref, wout_ref,
                    o_ref, buf_ref, s_ref, sz_ref, xn_ref, y_ref, wb_ref, *, tm, e, rc, ch):
    t = pl.program_id(1)

    @pl.when((pl.program_id(0) == 0) & (t == 0))
    def _():
        for j in range(CONV_W):
            wb_ref[j] = jnp.broadcast_to(wdw_ref[j:j + 1, :], (SUBLANES, e))

    @pl.when(t == 0)
    def _():
        s_ref[0:HALO, :] = jnp.zeros((HALO, e), F32)

    xn_ref[...] = _rms(x_ref[...], g_ref[...]).astype(BF16)
    for c in range(e // ch):
        xn = xn_ref[...]
        a = jnp.dot(xn, win_ref[:, c * ch:(c + 1) * ch], preferred_element_type=F32)
        gb = jnp.dot(xn, win_ref[:, e + c * ch:e + (c + 1) * ch], preferred_element_type=F32)
        z = jnp.dot(xn, win_ref[:, 2 * e + c * ch:2 * e + (c + 1) * ch], preferred_element_type=F32)
        s_ref[HALO:HALO + tm, c * ch:(c + 1) * ch] = a * jax.nn.sigmoid(gb)
        sz_ref[:, c * ch:(c + 1) * ch] = _silu(z)

    @pl.when(t == pl.num_programs(1) - 1)
    def _():
        buf_ref[...] = s_ref[tm:tm + HALO, :]

    groups = _conv_tap_groups()
    ncb = e // LANES
    nv = rc // SUBLANES

    def row_chunk(ci, _):
        r0 = pl.multiple_of(ci * rc, rc)

        def conv_cb(cb, s1):
            c0 = pl.multiple_of(cb * LANES, LANES)
            w3 = s_ref[pl.ds(r0, rc + HALO), pl.ds(c0, LANES)].reshape(nv + HALO // SUBLANES, SUBLANES, LANES)
            acc = jnp.broadcast_to(bdw_ref[:, pl.ds(c0, LANES)], (rc, LANES))
            for sh, taps in sorted(groups.items()):
                n = nv if sh == 0 else nv + 1
                u = None
                for j, a0 in taps:
                    v0 = a0 // SUBLANES
                    term = wb_ref[j, :, pl.ds(c0, LANES)][None] * w3[v0:v0 + n]
                    u = term if u is None else u + term
                u = u.reshape(n * SUBLANES, LANES)
                acc = acc + (u if sh == 0 else u[sh:sh + rc])
            s_ref[pl.ds(r0, rc), pl.ds(c0, LANES)] = acc
            return s1 + acc

        s1 = lax.fori_loop(0, ncb, conv_cb, jnp.zeros((rc, LANES), F32))
        mu = jnp.sum(s1, axis=-1, keepdims=True) * (1.0 / e)

        def var_cb(cb, s2):
            c0 = pl.multiple_of(cb * LANES, LANES)
            d = s_ref[pl.ds(r0, rc), pl.ds(c0, LANES)] - mu
            return s2 + d * d

        s2 = lax.fori_loop(0, ncb, var_cb, jnp.zeros((rc, LANES), F32))
        rstd = lax.rsqrt(jnp.sum(s2, axis=-1, keepdims=True) * (1.0 / e) + EPS)

        def norm_cb(cb, _):
            c0 = pl.multiple_of(cb * LANES, LANES)
            cv = s_ref[pl.ds(r0, rc), pl.ds(c0, LANES)]
            yn = (cv - mu) * rstd * lng_ref[:, pl.ds(c0, LANES)] + lnb_ref[:, pl.ds(c0, LANES)]
            y = _silu(yn) * sz_ref[pl.ds(r0, rc), pl.ds(c0, LANES)]
            y_ref[pl.ds(r0, rc), pl.ds(c0, LANES)] = y.astype(BF16)
            return 0

        lax.fori_loop(0, ncb, norm_cb, 0)
        return 0

    lax.fori_loop(0, tm // rc, row_chunk, 0)
    s_ref[0:HALO, :] = s_ref[tm:tm + HALO, :]
    o_ref[...] = x_ref[...] + jnp.dot(y_ref[...], wout_ref[...], preferred_element_type=F32)


def _a_layer_prompt(x, g, w_in, w_dw, b_dw, ln_g, ln_b, w_out, *, tm=256, rc=64, ch=512):
    b, l, d = x.shape
    e = w_dw.shape[-1]
    assert l % tm == 0 and tm % rc == 0 and tm >= HALO and e % ch == 0
    kern = functools.partial(_a_layer_kernel, tm=tm, e=e, rc=rc, ch=ch)
    row = lambda w: w.reshape(1, -1)
    return pl.pallas_call(
        kern,
        grid=(b, l // tm),
        in_specs=[
            pl.BlockSpec((None, tm, d), lambda i, t: (i, t, 0)),
            _full((1, d)), _full((d, 3 * e)), _full((CONV_W, e)),
            _full((1, e)), _full((1, e)), _full((1, e)), _full((e, d)),
        ],
        out_specs=[
            pl.BlockSpec((None, tm, d), lambda i, t: (i, t, 0)),
            pl.BlockSpec((None, HALO, e), lambda i, t: (i, 0, 0)),
        ],
        out_shape=[jax.ShapeDtypeStruct((b, l, d), F32), jax.ShapeDtypeStruct((b, HALO, e), F32)],
        scratch_shapes=[
            pltpu.VMEM((HALO + tm, e), F32), pltpu.VMEM((tm, e), F32),
            pltpu.VMEM((tm, d), BF16), pltpu.VMEM((tm, e), BF16),
            pltpu.VMEM((CONV_W, SUBLANES, e), F32),
        ],
        compiler_params=_cparams("arbitrary", "arbitrary"),
        name="a_layer_prompt",
    )(x, row(g), w_in, w_dw, row(b_dw), row(ln_g), row(ln_b), w_out)


def _kv_kernel(*refs, extras):
    if extras:
        x_ref, g_ref, wk_ref, wv_ref, wvt_ref, k_ref, v_ref, kb_ref, vt_ref, km_ref = refs
    else:
        x_ref, g_ref, wk_ref, wv_ref, k_ref, v_ref = refs
    hk = _rms(x_ref[...], g_ref[...]).astype(BF16)
    k = jnp.dot(hk, wk_ref[...], preferred_element_type=F32)
    k_ref[...] = k
    v_ref[...] = jnp.dot(hk, wv_ref[...], preferred_element_type=F32)
    if extras:
        kb_ref[...] = k.astype(BF16)
        vt_ref[...] = lax.dot_general(wvt_ref[...], hk, _NT, preferred_element_type=F32).astype(BF16)
        km_ref[...] = jnp.sum(k, axis=0, keepdims=True) * (1.0 / MOBA_BLOCK)


def _kv_proj(x, g, w_k, w_v, w_vt=None):
    b, l, d = x.shape
    a = w_k.shape[1]
    extras = w_vt is not None
    tm = MOBA_BLOCK if extras else l
    assert l % tm == 0
    xspec = pl.BlockSpec((None, tm, d), lambda i, t: (i, t, 0))
    ospec = pl.BlockSpec((None, tm, a), lambda i, t: (i, t, 0))
    in_specs = [xspec, _full((1, d)), _full((d, a)), _full((d, a))]
    out_specs = [ospec, ospec]
    out_shape = [jax.ShapeDtypeStruct((b, l, a), F32)] * 2
    args = [x, g.reshape(1, d), w_k, w_v]
    if extras:
        in_specs.append(_full((a, d)))
        args.append(w_vt)
        out_specs += [ospec, pl.BlockSpec((None, a, tm), lambda i, t: (i, 0, t)),
                      pl.BlockSpec((None, None, 1, a), lambda i, t: (i, t, 0, 0))]
        out_shape += [jax.ShapeDtypeStruct((b, l, a), BF16), jax.ShapeDtypeStruct((b, a, l), BF16),
                      jax.ShapeDtypeStruct((b, l // tm, 1, a), F32)]
    return pl.pallas_call(
        functools.partial(_kv_kernel, extras=extras),
        grid=(b, l // tm), in_specs=in_specs, out_specs=out_specs, out_shape=out_shape,
        compiler_params=_cparams("parallel", "parallel"), name="kv_proj",
    )(*args)


def _qz_kernel(x_ref, g_ref, w_ref, q_ref, qs_ref, sz_ref, *, a):
    xn = _rms(x_ref[...], g_ref[...]).astype(BF16)
    q = jnp.dot(xn, w_ref[:, 0:a], preferred_element_type=F32)
    z = jnp.dot(xn, w_ref[:, a:2 * a], preferred_element_type=F32)
    q_ref[...] = q
    qs_ref[...] = (q * (HEAD_DIM ** -0.5)).astype(BF16)
    sz_ref[...] = _silu(z)


def _qz_proj(x, g, w_in, *, tm):
    b, l, d = x.shape
    a = w_in.shape[1] // 2
    assert l % tm == 0
    xspec = pl.BlockSpec((None, tm, d), lambda i, t: (i, t, 0))
    ospec = pl.BlockSpec((None, tm, a), lambda i, t: (i, t, 0))
    return pl.pallas_call(
        functools.partial(_qz_kernel, a=a),
        grid=(b, l // tm),
        in_specs=[xspec, _full((1, d)), _full((d, 2 * a))],
        out_specs=[ospec, ospec, ospec],
        out_shape=[jax.ShapeDtypeStruct((b, l, a), F32), jax.ShapeDtypeStruct((b, l, a), BF16),
                   jax.ShapeDtypeStruct((b, l, a), F32)],
        compiler_params=_cparams("parallel", "parallel"), name="qz_proj",
    )(x, g.reshape(1, d), w_in)


def _out_kernel(o_ref, sz_ref, x_ref, w_ref, gf_ref, y_ref, *, final):
    y = (o_ref[...] * sz_ref[...]).astype(BF16)
    xo = x_ref[...] + jnp.dot(y, w_ref[...], preferred_element_type=F32)
    y_ref[...] = _rms(xo, gf_ref[...]) if final else xo


def _out_proj(o, sz, x, w_out, g_final, *, final, tm):
    b, l, d = x.shape
    a = o.shape[-1]
    assert l % tm == 0
    aspec = pl.BlockSpec((None, tm, a), lambda i, t: (i, t, 0))
    xspec = pl.BlockSpec((None, tm, d), lambda i, t: (i, t, 0))
    return pl.pallas_call(
        functools.partial(_out_kernel, final=final),
        grid=(b, l // tm),
        in_specs=[aspec, aspec, xspec, _full((a, d)), _full((1, d))],
        out_specs=xspec,
        out_shape=jax.ShapeDtypeStruct((b, l, d), F32),
        compiler_params=_cparams("parallel", "parallel"), name="out_proj",
    )(o, sz, x, w_out, g_final.reshape(1, d))


def _moba_prompt_kernel(q_ref, qs_ref, k_ref, vt_ref, km_ref, o_ref, sel_ref, qh_ref, acc_ref, *, nblk, sb, nh):
    i = pl.program_id(2)
    blk = MOBA_BLOCK
    w = sb * blk
    lane = lax.broadcasted_iota(jnp.int32, (1, LANES), 1)
    n_iota = lax.broadcasted_iota(jnp.int32, (nblk, blk), 0)
    pair = lambda hh: slice((hh // 2) * LANES, (hh // 2 + 1) * LANES)
    for hh in range(nh):
        hmask = (lane >> 6) == hh % 2
        g = lax.dot_general(km_ref[:, pair(hh)], jnp.where(hmask, q_ref[:, pair(hh)], 0.0), _NT,
                            precision=lax.Precision.HIGHEST, preferred_element_type=F32)
        g = jnp.where(n_iota < i, g, NEG_INF)
        sel = jnp.zeros((nblk, blk), F32)
        for _ in range(MOBA_TOPK):
            m = jnp.max(g, axis=0, keepdims=True)
            idx = jnp.min(jnp.where(g == m, n_iota, nblk), axis=0, keepdims=True)
            hit = n_iota == idx
            sel = jnp.where(hit & (m > NEG_INF), 1.0, sel)
            g = jnp.where(hit, NEG_INF, g)
        sel_ref[hh] = sel
        qs = qs_ref[:, pair(hh)]
        qh_ref[hh] = jnp.where(hmask, qs, jnp.zeros_like(qs))

    def scores(start, rows):
        return [lax.dot_general(k_ref[pl.ds(start, rows), pair(hh)], qh_ref[hh], _NT, preferred_element_type=F32)
                for hh in range(nh)]

    k_iota = lax.broadcasted_iota(jnp.int32, (blk, blk), 0)
    q_iota = lax.broadcasted_iota(jnp.int32, (blk, blk), 1)
    own = pl.multiple_of(i * blk, blk)
    sts = scores(own, blk)
    stats = []
    for hh in range(nh):
        st = jnp.where(k_iota <= q_iota, sts[hh], NEG_INF)
        m = jnp.max(st, axis=0, keepdims=True)
        p = jnp.exp(st - m)
        stats += [m, jnp.sum(p, axis=0, keepdims=True)]
        vt = vt_ref[hh * HEAD_DIM:(hh + 1) * HEAD_DIM, pl.ds(own, blk)]
        acc_ref[hh] = jnp.dot(vt, p.astype(BF16), preferred_element_type=F32)

    def body(t, carry):
        start = pl.multiple_of(t * w, w)
        sts = scores(start, w)
        out = []
        for hh in range(nh):
            m, l = carry[2 * hh], carry[2 * hh + 1]
            st = sts[hh]
            sels = [sel_ref[hh, pl.ds(t * sb + j, 1), :] > 0.0 for j in range(sb)]
            m_new = m
            for j in range(sb):
                mj = jnp.max(st[j * blk:(j + 1) * blk], axis=0, keepdims=True)
                m_new = jnp.maximum(m_new, jnp.where(sels[j], mj, NEG_INF))
            alpha = jnp.exp(m - m_new)
            l = alpha * l
            ps = []
            for j in range(sb):
                p = jnp.where(sels[j], jnp.exp(st[j * blk:(j + 1) * blk] - m_new), 0.0)
                l = l + jnp.sum(p, axis=0, keepdims=True)
                ps.append(p.astype(BF16))
            vt = vt_ref[hh * HEAD_DIM:(hh + 1) * HEAD_DIM, pl.ds(start, w)]
            acc_ref[hh] = alpha * acc_ref[hh] + jnp.dot(vt, jnp.concatenate(ps, axis=0),
                                                        preferred_element_type=F32)
            out += [m_new, l]
        return tuple(out)

    stats = lax.fori_loop(0, (i + sb - 1) // sb, body, tuple(stats))
    ot = jnp.concatenate([acc_ref[hh] / stats[2 * hh + 1] for hh in range(nh)], axis=0)
    o_ref[...] = ot.T


def _moba_prompt(q, qs, kb, vt, km, *, sb=2, nh=8):
    b, l, a = q.shape
    blk = MOBA_BLOCK
    nblk = l // blk
    wd = nh * HEAD_DIM
    assert l % blk == 0 and nblk % sb == 0 and a % wd == 0 and wd % LANES == 0 and LANES == 2 * HEAD_DIM
    qspec = pl.BlockSpec((None, blk, wd), lambda i, h, t: (i, t, h))
    return pl.pallas_call(
        functools.partial(_moba_prompt_kernel, nblk=nblk, sb=sb, nh=nh),
        grid=(b, a // wd, nblk),
        in_specs=[
            qspec, qspec,
            pl.BlockSpec((None, l, wd), lambda i, h, t: (i, 0, h)),
            pl.BlockSpec((None, wd, l), lambda i, h, t: (i, h, 0)),
            pl.BlockSpec((None, nblk, wd), lambda i, h, t: (i, 0, h)),
        ],
        out_specs=qspec,
        out_shape=jax.ShapeDtypeStruct((b, l, a), F32),
        scratch_shapes=[
            pltpu.VMEM((nh, nblk, blk), F32), pltpu.VMEM((nh, blk, LANES), BF16),
            pltpu.VMEM((nh, HEAD_DIM, blk), F32),
        ],
        compiler_params=_cparams("parallel", "parallel", "arbitrary"),
        name="moba_prompt",
    )(q, qs, kb, vt, km)


def _in_glu_kernel(x_ref, g_ref, wa_ref, wg_ref, wz_ref, glu_ref, sz_ref):
    xn = _rms(x_ref[...], g_ref[...]).astype(BF16)
    a = jnp.dot(xn, wa_ref[...], preferred_element_type=F32)
    gb = jnp.dot(xn, wg_ref[...], preferred_element_type=F32)
    z = jnp.dot(xn, wz_ref[...], preferred_element_type=F32)
    glu_ref[...] = a * jax.nn.sigmoid(gb)
    sz_ref[...] = _silu(z)


def _in_glu(x, g, w_in, *, ch=512):
    n, d = x.shape
    e = w_in.shape[1] // 3
    nc = e // ch
    wspec = lambda k: pl.BlockSpec((d, ch), lambda c: (0, c + k * nc))
    ospec = pl.BlockSpec((n, ch), lambda c: (0, c))
    return pl.pallas_call(
        _in_glu_kernel,
        grid=(nc,),
        in_specs=[_full((n, d)), _full((1, d)), wspec(0), wspec(1), wspec(2)],
        out_specs=[ospec, ospec],
        out_shape=[jax.ShapeDtypeStruct((n, e), F32)] * 2,
        compiler_params=_cparams("parallel"), name="in_glu_sample",
    )(x, g.reshape(1, d), w_in, w_in, w_in)


def _dec_conv_kernel(st_ref, glu_ref, sz_ref, wdw_ref, bdw_ref, lng_ref, lnb_ref,
                     y_ref, ns_ref, conv_ref, *, bt):
    nh = CONV_W - 1
    w_hist = wdw_ref[0:nh, :]
    for i in range(bt):
        conv_ref[i:i + 1, :] = jnp.sum(st_ref[i] * w_hist, axis=0, keepdims=True)
        ns_ref[i, 0:nh - 1, :] = st_ref[i, 1:nh, :]
        ns_ref[i, nh - 1:nh, :] = glu_ref[i:i + 1, :]
    conv = conv_ref[...] + glu_ref[...] * wdw_ref[nh:nh + 1, :] + bdw_ref[...]
    mu = jnp.mean(conv, axis=-1, keepdims=True)
    d = conv - mu
    yn = d * lax.rsqrt(jnp.mean(d * d, axis=-1, keepdims=True) + EPS) * lng_ref[...] + lnb_ref[...]
    y_ref[...] = (_silu(yn) * sz_ref[...]).astype(BF16)


def _dec_conv(state, glu, sz, w_dw, b_dw, ln_g, ln_b, *, bt=16):
    n, nh, e = state.shape
    assert n % bt == 0 and nh == CONV_W - 1
    rspec = pl.BlockSpec((bt, e), lambda i: (i, 0))
    sspec = pl.BlockSpec((bt, nh, e), lambda i: (i, 0, 0))
    row = lambda w: w.reshape(1, e)
    return pl.pallas_call(
        functools.partial(_dec_conv_kernel, bt=bt),
        grid=(n // bt,),
        in_specs=[sspec, rspec, rspec, _full((CONV_W, e)), _full((1, e)), _full((1, e)), _full((1, e))],
        out_specs=[rspec, sspec],
        out_shape=[jax.ShapeDtypeStruct((n, e), BF16), jax.ShapeDtypeStruct((n, nh, e), F32)],
        scratch_shapes=[pltpu.VMEM((bt, e), F32)],
        compiler_params=_cparams("parallel"), name="conv_sample",
    )(state, glu, sz, w_dw, row(b_dw), row(ln_g), row(ln_b))


def _res_proj_kernel(y_ref, x_ref, w_ref, o_ref):
    o_ref[...] = x_ref[...] + jnp.dot(y_ref[...], w_ref[...], preferred_element_type=F32)


def _res_proj(y, x, w):
    n, d = x.shape
    e = y.shape[1]
    return pl.pallas_call(
        _res_proj_kernel, grid=(1,),
        in_specs=[_full((n, e)), _full((n, d)), _full((e, d))],
        out_specs=_full((n, d)), out_shape=jax.ShapeDtypeStruct((n, d), F32),
        compiler_params=_cparams("arbitrary"), name="res_proj_sample",
    )(y, x, w)


def _head_rows(shape):
    r = lax.broadcasted_iota(jnp.int32, shape, 0)
    c = lax.broadcasted_iota(jnp.int32, shape, 1)
    return r == (c >> 6)


def _dec_scores_kernel(pt_ref, q_ref, *refs, pg, nh):
    del pt_ref
    k_refs, (sc_ref, km_ref) = refs[:pg], refs[pg:]
    page = k_refs[0].shape[0]
    a = q_ref.shape[-1]
    qrows = jnp.where(_head_rows((LANES, a)), q_ref[...], 0.0).astype(BF16)
    ppb = MOBA_BLOCK // page
    ksum = None
    for p in range(pg):
        kp = k_refs[p][...]
        s = lax.dot_general(kp.astype(BF16), qrows, _NT, preferred_element_type=F32)
        sc_ref[p] = s.T[0:nh, :] * (HEAD_DIM ** -0.5)
        part = jnp.sum(kp, axis=0, keepdims=True)
        ksum = part if p % ppb == 0 else ksum + part
        if p % ppb == ppb - 1:
            km_ref[p // ppb:p // ppb + 1, :] = ksum * (1.0 / MOBA_BLOCK)


def _dec_scores(page_table, q, cache_k, *, pg=8):
    bd, npg = page_table.shape
    _, page, a = cache_k.shape
    nh = a // HEAD_DIM
    assert npg % pg == 0 and MOBA_BLOCK % page == 0 and pg % (MOBA_BLOCK // page) == 0 and page == LANES
    bps = pg * page // MOBA_BLOCK
    kspec = lambda p: pl.BlockSpec((None, page, a), lambda b, j, pt: (pt[b, j * pg + p], 0, 0))
    grid_spec = pltpu.PrefetchScalarGridSpec(
        num_scalar_prefetch=1, grid=(bd, npg // pg),
        in_specs=[pl.BlockSpec((None, 1, a), lambda b, j, pt: (b, 0, 0))] + [kspec(p) for p in range(pg)],
        out_specs=[pl.BlockSpec((None, pg, nh, page), lambda b, j, pt: (b, j, 0, 0)),
                   pl.BlockSpec((None, None, bps, a), lambda b, j, pt: (b, j, 0, 0))],
    )
    return pl.pallas_call(
        functools.partial(_dec_scores_kernel, pg=pg, nh=nh),
        grid_spec=grid_spec,
        out_shape=[jax.ShapeDtypeStruct((bd, npg, nh, page), F32),
                   jax.ShapeDtypeStruct((bd, npg // pg, bps, a), F32)],
        compiler_params=_cparams("parallel", "arbitrary"), name="scores_sample",
    )(page_table, q, *([cache_k] * pg))


def _dec_pv_kernel(pt_ref, sc_ref, km_ref, q_ref, kn_ref, vn_ref, *refs, pg, nh, nblk):
    del pt_ref
    v_refs, (o_ref, p_ref, pown_ref, acc_ref) = refs[:pg], refs[pg:]
    j = pl.program_id(1)
    page = v_refs[0].shape[0]
    a = q_ref.shape[-1]
    ppb = MOBA_BLOCK // page
    hrows = _head_rows((nh, a))

    @pl.when(j == 0)
    def _():
        q = q_ref[...]
        g = [jnp.sum(jnp.where(hrows, km_ref[n:n + 1, :] * q, 0.0), axis=-1, keepdims=True)
             for n in range(nblk)]
        sel = [jnp.zeros((nh, 1), F32) for _ in range(nblk)]
        for _ in range(min(MOBA_TOPK, nblk)):
            m = functools.reduce(jnp.maximum, g)
            idx = functools.reduce(jnp.minimum, [jnp.where(g[n] == m, n, nblk) for n in range(nblk)])
            for n in range(nblk):
                hit = idx == n
                sel[n] = jnp.where(hit, 1.0, sel[n])
                g[n] = jnp.where(hit, NEG_INF, g[n])
        s_own = jnp.sum(jnp.where(hrows, kn_ref[...] * q, 0.0), axis=-1, keepdims=True) * (HEAD_DIM ** -0.5)
        m = s_own
        sc = []
        for pgi in range(nblk * ppb):
            s = jnp.where(sel[pgi // ppb] > 0.0, sc_ref[pgi], NEG_INF)
            sc.append(s)
            m = jnp.maximum(m, jnp.max(s, axis=-1, keepdims=True))
        p_own = jnp.exp(s_own - m)
        l = p_own
        for pgi in range(nblk * ppb):
            p = jnp.exp(sc[pgi] - m)
            sc[pgi] = p
            l = l + jnp.sum(p, axis=-1, keepdims=True)
        inv = 1.0 / l
        for pgi in range(nblk * ppb):
            p_ref[pgi] = sc[pgi] * inv
        pown_ref[...] = jnp.broadcast_to(p_own * inv, pown_ref.shape)
        acc_ref[...] = jnp.zeros_like(acc_ref)

    acc = acc_ref[...]
    for p in range(pg):
        acc = acc + jnp.dot(p_ref[j * pg + p].astype(BF16), v_refs[p][...].astype(BF16),
                            preferred_element_type=F32)
    acc_ref[...] = acc

    @pl.when(j == pl.num_programs(1) - 1)
    def _():
        full = acc_ref[...] + pown_ref[:, 0:1] * vn_ref[...]
        o_ref[...] = jnp.sum(jnp.where(hrows, full, 0.0), axis=0, keepdims=True)


def _dec_pv(page_table, scores, kmean, q, k_new, v_new, cache_v, *, pg=8):
    bd, npg = page_table.shape
    _, page, a = cache_v.shape
    nh = a // HEAD_DIM
    nblk = kmean.shape[1]
    assert npg % pg == 0 and nblk * MOBA_BLOCK == npg * page
    vspec = lambda p: pl.BlockSpec((None, page, a), lambda b, j, pt: (pt[b, j * pg + p], 0, 0))
    rowspec = pl.BlockSpec((None, 1, a), lambda b, j, pt: (b, 0, 0))
    grid_spec = pltpu.PrefetchScalarGridSpec(
        num_scalar_prefetch=1, grid=(bd, npg // pg),
        in_specs=[pl.BlockSpec((None, npg, nh, page), lambda b, j, pt: (b, 0, 0, 0)),
                  pl.BlockSpec((None, nblk, a), lambda b, j, pt: (b, 0, 0)),
                  rowspec, rowspec, rowspec] + [vspec(p) for p in range(pg)],
        out_specs=rowspec,
        scratch_shapes=[pltpu.VMEM((npg, nh, page), F32), pltpu.VMEM((nh, LANES), F32),
                        pltpu.VMEM((nh, a), F32)],
    )
    return pl.pallas_call(
        functools.partial(_dec_pv_kernel, pg=pg, nh=nh, nblk=nblk),
        grid_spec=grid_spec,
        out_shape=jax.ShapeDtypeStruct((bd, 1, a), F32),
        compiler_params=_cparams("parallel", "arbitrary"), name="pv_sample",
    )(page_table, scores, kmean, q, k_new, v_new, *([cache_v] * pg))


def kernel(x_prompt, x_sample, state_conv, cache_k, cache_v, page_table, g_a, w_in_a, w_dw, b_dw, ln_g_a, ln_b_a, w_out_a, g_kv, w_k, w_v, g_b, w_in_b, w_out_b, g_final):
    n_a, n_b = w_in_a.shape[0], w_in_b.shape[0]
    bp, lp, d = x_prompt.shape
    bd, ld, _ = x_sample.shape
    n_pool, page, nh, hd = cache_k.shape
    a = nh * hd
    nhist = CONV_W - 1
    assert ld == 1 and hd == HEAD_DIM and lp >= nhist
    assert (page_table.shape[1] * page) % MOBA_BLOCK == 0

    w_in_a_h, w_out_a_h = w_in_a.astype(BF16), w_out_a.astype(BF16)
    w_k_h, w_v_h, w_vt_h = w_k.astype(BF16), w_v.astype(BF16), w_v.T.astype(BF16)
    w_in_b_h, w_out_b_h = w_in_b.astype(BF16), w_out_b.astype(BF16)

    x = x_prompt
    bufs = []
    for i in range(n_a):
        x, buf = _a_layer_prompt(x, g_a[i], w_in_a_h[i], w_dw[i], b_dw[i], ln_g_a[i], ln_b_a[i], w_out_a_h[i])
        bufs.append(buf[:, HALO - nhist:, :])
    k_p, v_p, kb, vt, km = _kv_proj(x, g_kv, w_k_h, w_v_h, w_vt_h)
    km = km.reshape(bp, lp // MOBA_BLOCK, a)
    for j in range(n_b):
        q, qs, sz = _qz_proj(x, g_b[j], w_in_b_h[j], tm=512)
        o = _moba_prompt(q, qs, kb, vt, km)
        x = _out_proj(o, sz, x, w_out_b_h[j], g_final, final=(j == n_b - 1), tm=512)
    y_prompt = x
    conv_prompt = jnp.stack(bufs)

    xs = x_sample.reshape(bd, d)
    new_states = []
    for i in range(n_a):
        glu, sz = _in_glu(xs, g_a[i], w_in_a_h[i])
        y, ns = _dec_conv(state_conv[i], glu, sz, w_dw[i], b_dw[i], ln_g_a[i], ln_b_a[i])
        xs = _res_proj(y, xs, w_out_a_h[i])
        new_states.append(ns)
    k_s, v_s = _kv_proj(xs.reshape(1, bd, d), g_kv, w_k_h, w_v_h)
    ck = cache_k.reshape(n_pool, page, a)
    cv = cache_v.reshape(n_pool, page, a)
    k_row, v_row = k_s.reshape(bd, 1, a), v_s.reshape(bd, 1, a)
    xs3 = xs.reshape(1, bd, d)
    for j in range(n_b):
        q, _, sz = _qz_proj(xs3, g_b[j], w_in_b_h[j], tm=bd)
        q_row = q.reshape(bd, 1, a)
        scores, kmean = _dec_scores(page_table, q_row, ck)
        kmean = kmean.reshape(bd, -1, a)
        o = _dec_pv(page_table, scores, kmean, q_row, k_row, v_row, cv)
        xs3 = _out_proj(o.reshape(1, bd, a), sz, xs3, w_out_b_h[j], g_final, final=(j == n_b - 1), tm=bd)
    y_sample = xs3.reshape(bd, 1, d)

    return (y_prompt, y_sample,
            k_p.reshape(bp, lp, nh, hd), v_p.reshape(bp, lp, nh, hd), conv_prompt,
            k_s.reshape(bd, 1, nh, hd), v_s.reshape(bd, 1, nh, hd), jnp.stack(new_states))
```

```python
import functools

import jax
import jax.numpy as jnp
from jax import lax
from jax.experimental import pallas as pl
from jax.experimental.pallas import tpu as pltpu

F32 = jnp.float32
BF16 = jnp.bfloat16

EPS = 1e-6
CONV_W = 31
HALO = 32
HEAD_DIM = 64
VT_ROWS = HEAD_DIM + 16
MOBA_BLOCK = 256
MOBA_TOPK = 3
LANES = 128
SUBLANES = 8
VMEM_LIMIT = 56 * 1024 * 1024
NEG_INF = float("-inf")

_NT = (((1,), (1,)), ((), ()))


def _cparams(*sem):
    return pltpu.CompilerParams(dimension_semantics=sem, vmem_limit_bytes=VMEM_LIMIT)


def _rms(x, g):
    return x * lax.rsqrt(jnp.mean(x * x, axis=-1, keepdims=True) + EPS) * g


def _silu(x):
    return x * jax.nn.sigmoid(x)


def _full(shape):
    return pl.BlockSpec(shape, lambda *_: (0,) * len(shape))


def _conv_tap_groups():
    groups = {}
    for j in range(CONV_W):
        off = HALO - (CONV_W - 1) + j
        groups.setdefault(off % SUBLANES, []).append((j, off - off % SUBLANES))
    return groups


def _a_layer_kernel(x_ref, g_ref, win_ref, wdw_ref, bdw_ref, lng_ref, lnb_ref, wout_ref,
                    o_ref, buf_ref, s_ref, sz_ref, xn_ref, y_ref, wb_ref, *, tm, e, rc, ch):
    t = pl.program_id(1)

    @pl.when((pl.program_id(0) == 0) & (t == 0))
    def _():
        for j in range(CONV_W):
            wb_ref[j] = jnp.broadcast_to(wdw_ref[j:j + 1, :], (SUBLANES, e))

    @pl.when(t == 0)
    def _():
        s_ref[0:HALO, :] = jnp.zeros((HALO, e), F32)

    xn_ref[...] = _rms(x_ref[...], g_ref[...]).astype(BF16)
    for c in range(e // ch):
        xn = xn_ref[...]
        a = jnp.dot(xn, win_ref[:, c * ch:(c + 1) * ch], preferred_element_type=F32)
        gb = jnp.dot(xn, win_ref[:, e + c * ch:e + (c + 1) * ch], preferred_element_type=F32)
        z = jnp.dot(xn, win_ref[:, 2 * e + c * ch:2 * e + (c + 1) * ch], preferred_element_type=F32)
        s_ref[HALO:HALO + tm, c * ch:(c + 1) * ch] = a * jax.nn.sigmoid(gb)
        sz_ref[:, c * ch:(c + 1) * ch] = _silu(z)

    @pl.when(t == pl.num_programs(1) - 1)
    def _():
        buf_ref[...] = s_ref[tm:tm + HALO, :]

    groups = _conv_tap_groups()
    ncb = e // LANES
    nv = rc // SUBLANES

    def row_chunk(ci, _):
        r0 = pl.multiple_of(ci * rc, rc)

        def conv_cb(cb, s1):
            c0 = pl.multiple_of(cb * LANES, LANES)
            w3 = s_ref[pl.ds(r0, rc + HALO), pl.ds(c0, LANES)].reshape(nv + HALO // SUBLANES, SUBLANES, LANES)
            acc = jnp.broadcast_to(bdw_ref[:, pl.ds(c0, LANES)], (rc, LANES))
            for sh, taps in sorted(groups.items()):
                n = nv if sh == 0 else nv + 1
                u = None
                for j, a0 in taps:
                    v0 = a0 // SUBLANES
                    term = wb_ref[j, :, pl.ds(c0, LANES)][None] * w3[v0:v0 + n]
                    u = term if u is None else u + term
                u = u.reshape(n * SUBLANES, LANES)
                acc = acc + (u if sh == 0 else u[sh:sh + rc])
            s_ref[pl.ds(r0, rc), pl.ds(c0, LANES)] = acc
            return s1 + acc

        s1 = lax.fori_loop(0, ncb, conv_cb, jnp.zeros((rc, LANES), F32))
        mu = jnp.sum(s1, axis=-1, keepdims=True) * (1.0 / e)

        def var_cb(cb, s2):
            c0 = pl.multiple_of(cb * LANES, LANES)
            d = s_ref[pl.ds(r0, rc), pl.ds(c0, LANES)] - mu
            return s2 + d * d

        s2 = lax.fori_loop(0, ncb, var_cb, jnp.zeros((rc, LANES), F32))
        rstd = lax.rsqrt(jnp.sum(s2, axis=-1, keepdims=True) * (1.0 / e) + EPS)

        def norm_cb(cb, _):
            c0 = pl.multiple_of(cb * LANES, LANES)
            cv = s_ref[pl.ds(r0, rc), pl.ds(c0, LANES)]
            yn = (cv - mu) * rstd * lng_ref[:, pl.ds(c0, LANES)] + lnb_ref[:, pl.ds(c0, LANES)]
            y = _silu(yn) * sz_ref[pl.ds(r0, rc), pl.ds(c0, LANES)]
            y_ref[pl.ds(r0, rc), pl.ds(c0, LANES)] = y.astype(BF16)
            return 0

        lax.fori_loop(0, ncb, norm_cb, 0)
        return 0

    lax.fori_loop(0, tm // rc, row_chunk, 0)
    s_ref[0:HALO, :] = s_ref[tm:tm + HALO, :]
    o_ref[...] = x_ref[...] + jnp.dot(y_ref[...], wout_ref[...], preferred_element_type=F32)


def _a_layer_prompt(x, g, w_in, w_dw, b_dw, ln_g, ln_b, w_out, *, tm=256, rc=64, ch=512):
    b, l, d = x.shape
    e = w_dw.shape[-1]
    assert l % tm == 0 and tm % rc == 0 and tm >= HALO and e % ch == 0
    kern = functools.partial(_a_layer_kernel, tm=tm, e=e, rc=rc, ch=ch)
    row = lambda w: w.reshape(1, -1)
    return pl.pallas_call(
        kern,
        grid=(b, l // tm),
        in_specs=[
            pl.BlockSpec((None, tm, d), lambda i, t: (i, t, 0)),
            _full((1, d)), _full((d, 3 * e)), _full((CONV_W, e)),
            _full((1, e)), _full((1, e)), _full((1, e)), _full((e, d)),
        ],
        out_specs=[
            pl.BlockSpec((None, tm, d), lambda i, t: (i, t, 0)),
            pl.BlockSpec((None, HALO, e), lambda i, t: (i, 0, 0)),
        ],
        out_shape=[jax.ShapeDtypeStruct((b, l, d), F32), jax.ShapeDtypeStruct((b, HALO, e), F32)],
        scratch_shapes=[
            pltpu.VMEM((HALO + tm, e), F32), pltpu.VMEM((tm, e), F32),
            pltpu.VMEM((tm, d), BF16), pltpu.VMEM((tm, e), BF16),
            pltpu.VMEM((CONV_W, SUBLANES, e), F32),
        ],
        compiler_params=_cparams("arbitrary", "arbitrary"),
        name="a_layer_prompt",
    )(x, row(g), w_in, w_dw, row(b_dw), row(ln_g), row(ln_b), w_out)


def _kv_kernel(*refs, extras):
    if extras:
        x_ref, g_ref, wk_ref, wv_ref, wvt_ref, k_ref, v_ref, kb_ref, vt_ref, km_ref = refs
    else:
        x_ref, g_ref, wk_ref, wv_ref, k_ref, v_ref = refs
    hk = _rms(x_ref[...], g_ref[...]).astype(BF16)
    k = jnp.dot(hk, wk_ref[...], preferred_element_type=F32)
    k_ref[...] = k
    v_ref[...] = jnp.dot(hk, wv_ref[...], preferred_element_type=F32)
    if extras:
        kb_ref[...] = k.astype(BF16)
        nh = vt_ref.shape[0]
        vt = lax.dot_general(wvt_ref[...], hk, _NT, preferred_element_type=F32)
        vt_ref[:, 0:HEAD_DIM, :] = vt.reshape(nh, HEAD_DIM, vt.shape[-1]).astype(BF16)
        vt_ref[:, HEAD_DIM:, :] = jnp.ones((nh, VT_ROWS - HEAD_DIM, vt.shape[-1]), BF16)
        km_ref[...] = jnp.sum(k, axis=0, keepdims=True) * (1.0 / MOBA_BLOCK)


def _kv_proj(x, g, w_k, w_v, w_vt=None):
    b, l, d = x.shape
    a = w_k.shape[1]
    extras = w_vt is not None
    tm = MOBA_BLOCK if extras else l
    assert l % tm == 0
    xspec = pl.BlockSpec((None, tm, d), lambda i, t: (i, t, 0))
    ospec = pl.BlockSpec((None, tm, a), lambda i, t: (i, t, 0))
    in_specs = [xspec, _full((1, d)), _full((d, a)), _full((d, a))]
    out_specs = [ospec, ospec]
    out_shape = [jax.ShapeDtypeStruct((b, l, a), F32)] * 2
    args = [x, g.reshape(1, d), w_k, w_v]
    if extras:
        in_specs.append(_full((a, d)))
        args.append(w_vt)
        nh = a // HEAD_DIM
        out_specs += [ospec, pl.BlockSpec((None, nh, VT_ROWS, tm), lambda i, t: (i, 0, 0, t)),
                      pl.BlockSpec((None, None, 1, a), lambda i, t: (i, t, 0, 0))]
        out_shape += [jax.ShapeDtypeStruct((b, l, a), BF16), jax.ShapeDtypeStruct((b, nh, VT_ROWS, l), BF16),
                      jax.ShapeDtypeStruct((b, l // tm, 1, a), F32)]
    return pl.pallas_call(
        functools.partial(_kv_kernel, extras=extras),
        grid=(b, l // tm), in_specs=in_specs, out_specs=out_specs, out_shape=out_shape,
        compiler_params=_cparams("parallel", "parallel"), name="kv_proj",
    )(*args)


def _qz_kernel(x_ref, g_ref, w_ref, q_ref, qs_ref, sz_ref, *, a):
    xn = _rms(x_ref[...], g_ref[...]).astype(BF16)
    q = jnp.dot(xn, w_ref[:, 0:a], preferred_element_type=F32)
    z = jnp.dot(xn, w_ref[:, a:2 * a], preferred_element_type=F32)
    q_ref[...] = q
    qs_ref[...] = (q * (HEAD_DIM ** -0.5)).astype(BF16)
    sz_ref[...] = _silu(z)


def _qz_proj(x, g, w_in, *, tm):
    b, l, d = x.shape
    a = w_in.shape[1] // 2
    assert l % tm == 0
    xspec = pl.BlockSpec((None, tm, d), lambda i, t: (i, t, 0))
    ospec = pl.BlockSpec((None, tm, a), lambda i, t: (i, t, 0))
    return pl.pallas_call(
        functools.partial(_qz_kernel, a=a),
        grid=(b, l // tm),
        in_specs=[xspec, _full((1, d)), _full((d, 2 * a))],
        out_specs=[ospec, ospec, ospec],
        out_shape=[jax.ShapeDtypeStruct((b, l, a), F32), jax.ShapeDtypeStruct((b, l, a), BF16),
                   jax.ShapeDtypeStruct((b, l, a), F32)],
        compiler_params=_cparams("parallel", "parallel"), name="qz_proj",
    )(x, g.reshape(1, d), w_in)


def _out_kernel(o_ref, sz_ref, x_ref, w_ref, gf_ref, y_ref, *, final):
    y = (o_ref[...] * sz_ref[...]).astype(BF16)
    xo = x_ref[...] + jnp.dot(y, w_ref[...], preferred_element_type=F32)
    y_ref[...] = _rms(xo, gf_ref[...]) if final else xo


def _out_proj(o, sz, x, w_out, g_final, *, final, tm):
    b, l, d = x.shape
    a = o.shape[-1]
    assert l % tm == 0
    aspec = pl.BlockSpec((None, tm, a), lambda i, t: (i, t, 0))
    xspec = pl.BlockSpec((None, tm, d), lambda i, t: (i, t, 0))
    return pl.pallas_call(
        functools.partial(_out_kernel, final=final),
        grid=(b, l // tm),
        in_specs=[aspec, aspec, xspec, _full((a, d)), _full((1, d))],
        out_specs=xspec,
        out_shape=jax.ShapeDtypeStruct((b, l, d), F32),
        compiler_params=_cparams("parallel", "parallel"), name="out_proj",
    )(o, sz, x, w_out, g_final.reshape(1, d))


def _moba_prompt_kernel(q_ref, qs_ref, k_ref, vt_ref, km_ref, o_ref, sel_ref, qh_ref, acc_ref, *, nblk, sb, nh):
    i = pl.program_id(2)
    blk = MOBA_BLOCK
    w = sb * blk
    lane = lax.broadcasted_iota(jnp.int32, (1, LANES), 1)
    n_iota = lax.broadcasted_iota(jnp.int32, (nblk, blk), 0)
    pair = lambda hh: slice((hh // 2) * LANES, (hh // 2 + 1) * LANES)
    for hh in range(nh):
        hmask = (lane >> 6) == hh % 2
        g = lax.dot_general(km_ref[:, pair(hh)], jnp.where(hmask, q_ref[:, pair(hh)], 0.0), _NT,
                            precision=lax.Precision.HIGHEST, preferred_element_type=F32)
        g = jnp.where(n_iota < i, g, NEG_INF)
        sel = jnp.zeros((nblk, blk), F32)
        for _ in range(MOBA_TOPK):
            m = jnp.max(g, axis=0, keepdims=True)
            idx = jnp.min(jnp.where(g == m, n_iota, nblk), axis=0, keepdims=True)
            hit = n_iota == idx
            sel = jnp.where(hit & (m > NEG_INF), 1.0, sel)
            g = jnp.where(hit, NEG_INF, g)
        sel_ref[hh] = sel
        qs = qs_ref[:, pair(hh)]
        qh_ref[hh] = jnp.where(hmask, qs, jnp.zeros_like(qs))

    def scores(start, rows):
        return [lax.dot_general(k_ref[pl.ds(start, rows), pair(hh)], qh_ref[hh], _NT, preferred_element_type=F32)
                for hh in range(nh)]

    k_iota = lax.broadcasted_iota(jnp.int32, (blk, blk), 0)
    q_iota = lax.broadcasted_iota(jnp.int32, (blk, blk), 1)
    own = pl.multiple_of(i * blk, blk)
    sts = scores(own, blk)
    maxes = []
    for hh in range(nh):
        st = jnp.where(k_iota <= q_iota, sts[hh], NEG_INF)
        m = jnp.max(st, axis=0, keepdims=True)
        maxes.append(m)
        acc_ref[hh] = jnp.dot(vt_ref[hh, :, pl.ds(own, blk)], jnp.exp(st - m).astype(BF16),
                              preferred_element_type=F32)

    def body(t, maxes):
        start = pl.multiple_of(t * w, w)
        sts = scores(start, w)
        out = []
        for hh in range(nh):
            m = maxes[hh]
            st = sts[hh]
            sels = [sel_ref[hh, pl.ds(t * sb + j, 1), :] > 0.0 for j in range(sb)]
            m_new = m
            for j in range(sb):
                mj = jnp.max(st[j * blk:(j + 1) * blk], axis=0, keepdims=True)
                m_new = jnp.maximum(m_new, jnp.where(sels[j], mj, NEG_INF))
            ps = [jnp.where(sels[j], jnp.exp(st[j * blk:(j + 1) * blk] - m_new), 0.0).astype(BF16)
                  for j in range(sb)]
            acc_ref[hh] = jnp.exp(m - m_new) * acc_ref[hh] + jnp.dot(
                vt_ref[hh, :, pl.ds(start, w)], jnp.concatenate(ps, axis=0), preferred_element_type=F32)
            out.append(m_new)
        return tuple(out)

    lax.fori_loop(0, (i + sb - 1) // sb, body, tuple(maxes))
    ot = jnp.concatenate([acc_ref[hh, 0:HEAD_DIM, :] / acc_ref[hh, HEAD_DIM:HEAD_DIM + 1, :] for hh in range(nh)],
                         axis=0)
    o_ref[...] = ot.T


def _moba_prompt(q, qs, kb, vt, km, *, sb=2, nh=8):
    b, l, a = q.shape
    blk = MOBA_BLOCK
    nblk = l // blk
    wd = nh * HEAD_DIM
    assert l % blk == 0 and nblk % sb == 0 and a % wd == 0 and wd % LANES == 0 and LANES == 2 * HEAD_DIM
    qspec = pl.BlockSpec((None, blk, wd), lambda i, h, t: (i, t, h))
    return pl.pallas_call(
        functools.partial(_moba_prompt_kernel, nblk=nblk, sb=sb, nh=nh),
        grid=(b, a // wd, nblk),
        in_specs=[
            qspec, qspec,
            pl.BlockSpec((None, l, wd), lambda i, h, t: (i, 0, h)),
            pl.BlockSpec((None, nh, VT_ROWS, l), lambda i, h, t: (i, h, 0, 0)),
            pl.BlockSpec((None, nblk, wd), lambda i, h, t: (i, 0, h)),
        ],
        out_specs=qspec,
        out_shape=jax.ShapeDtypeStruct((b, l, a), F32),
        scratch_shapes=[
            pltpu.VMEM((nh, nblk, blk), F32), pltpu.VMEM((nh, blk, LANES), BF16),
            pltpu.VMEM((nh, VT_ROWS, blk), F32),
        ],
        compiler_params=_cparams("parallel", "parallel", "arbitrary"),
        name="moba_prompt",
    )(q, qs, kb, vt, km)


def _in_glu_kernel(x_ref, g_ref, wa_ref, wg_ref, wz_ref, glu_ref, sz_ref):
    xn = _rms(x_ref[...], g_ref[...]).astype(BF16)
    a = jnp.dot(xn, wa_ref[...], preferred_element_type=F32)
    gb = jnp.dot(xn, wg_ref[...], preferred_element_type=F32)
    z = jnp.dot(xn, wz_ref[...], preferred_element_type=F32)
    glu_ref[...] = a * jax.nn.sigmoid(gb)
    sz_ref[...] = _silu(z)


def _in_glu(x, g, w_in, *, ch=512):
    n, d = x.shape
    e = w_in.shape[1] // 3
    nc = e // ch
    wspec = lambda k: pl.BlockSpec((d, ch), lambda c: (0, c + k * nc))
    ospec = pl.BlockSpec((n, ch), lambda c: (0, c))
    return pl.pallas_call(
        _in_glu_kernel,
        grid=(nc,),
        in_specs=[_full((n, d)), _full((1, d)), wspec(0), wspec(1), wspec(2)],
        out_specs=[ospec, ospec],
        out_shape=[jax.ShapeDtypeStruct((n, e), F32)] * 2,
        compiler_params=_cparams("parallel"), name="in_glu_sample",
    )(x, g.reshape(1, d), w_in, w_in, w_in)


def _dec_conv_kernel(st_ref, glu_ref, sz_ref, wdw_ref, bdw_ref, lng_ref, lnb_ref,
                     y_ref, ns_ref, conv_ref, *, bt):
    nh = CONV_W - 1
    w_hist = wdw_ref[0:nh, :]
    for i in range(bt):
        conv_ref[i:i + 1, :] = jnp.sum(st_ref[i] * w_hist, axis=0, keepdims=True)
        ns_ref[i, 0:nh - 1, :] = st_ref[i, 1:nh, :]
        ns_ref[i, nh - 1:nh, :] = glu_ref[i:i + 1, :]
    conv = conv_ref[...] + glu_ref[...] * wdw_ref[nh:nh + 1, :] + bdw_ref[...]
    mu = jnp.mean(conv, axis=-1, keepdims=True)
    d = conv - mu
    yn = d * lax.rsqrt(jnp.mean(d * d, axis=-1, keepdims=True) + EPS) * lng_ref[...] + lnb_ref[...]
    y_ref[...] = (_silu(yn) * sz_ref[...]).astype(BF16)


def _dec_conv(state, glu, sz, w_dw, b_dw, ln_g, ln_b, *, bt=16):
    n, nh, e = state.shape
    assert n % bt == 0 and nh == CONV_W - 1
    rspec = pl.BlockSpec((bt, e), lambda i: (i, 0))
    sspec = pl.BlockSpec((bt, nh, e), lambda i: (i, 0, 0))
    row = lambda w: w.reshape(1, e)
    return pl.pallas_call(
        functools.partial(_dec_conv_kernel, bt=bt),
        grid=(n // bt,),
        in_specs=[sspec, rspec, rspec, _full((CONV_W, e)), _full((1, e)), _full((1, e)), _full((1, e))],
        out_specs=[rspec, sspec],
        out_shape=[jax.ShapeDtypeStruct((n, e), BF16), jax.ShapeDtypeStruct((n, nh, e), F32)],
        scratch_shapes=[pltpu.VMEM((bt, e), F32)],
        compiler_params=_cparams("parallel"), name="conv_sample",
    )(state, glu, sz, w_dw, row(b_dw), row(ln_g), row(ln_b))


def _res_proj_kernel(y_ref, x_ref, w_ref, o_ref):
    o_ref[...] = x_ref[...] + jnp.dot(y_ref[...], w_ref[...], preferred_element_type=F32)


def _res_proj(y, x, w):
    n, d = x.shape
    e = y.shape[1]
    return pl.pallas_call(
        _res_proj_kernel, grid=(1,),
        in_specs=[_full((n, e)), _full((n, d)), _full((e, d))],
        out_specs=_full((n, d)), out_shape=jax.ShapeDtypeStruct((n, d), F32),
        compiler_params=_cparams("arbitrary"), name="res_proj_sample",
    )(y, x, w)


def _head_rows(shape):
    r = lax.broadcasted_iota(jnp.int32, shape, 0)
    c = lax.broadcasted_iota(jnp.int32, shape, 1)
    return r == (c >> 6)


def _page_rows_to_dense(ref, nh):
    page = ref.shape[0] // nh
    return jnp.concatenate([ref[pl.ds(h, page, stride=nh), :] for h in range(nh)], axis=-1)


def _query_rows(q_ref):
    a = q_ref.shape[-1]
    return jnp.where(_head_rows((LANES, a)), q_ref[...], 0.0).astype(BF16)


def _store_scores(sc_ref, s, nh, page):
    for p in range(s.shape[0] // page):
        sc_ref[p] = s[p * page:(p + 1) * page, :].T[0:nh, :] * (HEAD_DIM ** -0.5)


def _dec_gather_scores_kernel(pt_ref, q_ref, *refs, pg, nh):
    del pt_ref
    k_refs, (sc_ref, km_ref, kd_ref) = refs[:pg], refs[pg:]
    page = k_refs[0].shape[0] // nh
    ppb = MOBA_BLOCK // page
    ksum = None
    for p in range(pg):
        kp = _page_rows_to_dense(k_refs[p], nh)
        kd_ref[p * page:(p + 1) * page, :] = kp.astype(BF16)
        part = jnp.sum(kp, axis=0, keepdims=True)
        ksum = part if p % ppb == 0 else ksum + part
        if p % ppb == ppb - 1:
            km_ref[p // ppb:p // ppb + 1, :] = ksum * (1.0 / MOBA_BLOCK)
    s = lax.dot_general(kd_ref[...], _query_rows(q_ref), _NT, preferred_element_type=F32)
    _store_scores(sc_ref, s, nh, page)


def _dec_gather_scores(page_table, q, cache_k, *, pg=8):
    bd, npg = page_table.shape
    n_pool, page, nh, hd = cache_k.shape
    a = nh * hd
    assert npg % pg == 0 and MOBA_BLOCK % page == 0 and pg % (MOBA_BLOCK // page) == 0 and page == LANES
    assert nh % SUBLANES == 0
    bps = pg * page // MOBA_BLOCK
    rows = cache_k.reshape(n_pool, page * nh, hd)
    kspec = lambda p: pl.BlockSpec((None, page * nh, hd), lambda b, j, pt: (pt[b, j * pg + p], 0, 0))
    grid_spec = pltpu.PrefetchScalarGridSpec(
        num_scalar_prefetch=1, grid=(bd, npg // pg),
        in_specs=[pl.BlockSpec((None, 1, a), lambda b, j, pt: (b, 0, 0))] + [kspec(p) for p in range(pg)],
        out_specs=[pl.BlockSpec((None, pg, nh, page), lambda b, j, pt: (b, j, 0, 0)),
                   pl.BlockSpec((None, None, bps, a), lambda b, j, pt: (b, j, 0, 0)),
                   pl.BlockSpec((None, pg * page, a), lambda b, j, pt: (b, j, 0))],
    )
    return pl.pallas_call(
        functools.partial(_dec_gather_scores_kernel, pg=pg, nh=nh),
        grid_spec=grid_spec,
        out_shape=[jax.ShapeDtypeStruct((bd, npg, nh, page), F32),
                   jax.ShapeDtypeStruct((bd, npg // pg, bps, a), F32),
                   jax.ShapeDtypeStruct((bd, npg * page, a), BF16)],
        compiler_params=_cparams("parallel", "arbitrary"), name="gather_scores_sample",
    )(page_table, q, *([rows] * pg))


def _dec_scores_dense_kernel(q_ref, kd_ref, sc_ref, *, nh, page):
    s = lax.dot_general(kd_ref[...], _query_rows(q_ref), _NT, preferred_element_type=F32)
    _store_scores(sc_ref, s, nh, page)


def _dec_scores_dense(q, kd, *, nh, page):
    bd, lk, a = kd.shape
    return pl.pallas_call(
        functools.partial(_dec_scores_dense_kernel, nh=nh, page=page),
        grid=(bd,),
        in_specs=[pl.BlockSpec((None, 1, a), lambda b: (b, 0, 0)), pl.BlockSpec((None, lk, a), lambda b: (b, 0, 0))],
        out_specs=pl.BlockSpec((None, lk // page, nh, page), lambda b: (b, 0, 0, 0)),
        out_shape=jax.ShapeDtypeStruct((bd, lk // page, nh, page), F32),
        compiler_params=_cparams("parallel"), name="scores_dense_sample",
    )(q, kd)


def _dec_probs(sc_ref, km_ref, q, kn, hrows, nblk, ppb):
    nh = hrows.shape[0]
    g = [jnp.sum(jnp.where(hrows, km_ref[n:n + 1, :] * q, 0.0), axis=-1, keepdims=True) for n in range(nblk)]
    sel = [jnp.zeros((nh, 1), F32) for _ in range(nblk)]
    for _ in range(min(MOBA_TOPK, nblk)):
        m = functools.reduce(jnp.maximum, g)
        idx = functools.reduce(jnp.minimum, [jnp.where(g[n] == m, n, nblk) for n in range(nblk)])
        for n in range(nblk):
            hit = idx == n
            sel[n] = jnp.where(hit, 1.0, sel[n])
            g[n] = jnp.where(hit, NEG_INF, g[n])
    s_own = jnp.sum(jnp.where(hrows, kn * q, 0.0), axis=-1, keepdims=True) * (HEAD_DIM ** -0.5)
    m = s_own
    sc = []
    for pgi in range(nblk * ppb):
        s = jnp.where(sel[pgi // ppb] > 0.0, sc_ref[pgi], NEG_INF)
        sc.append(s)
        m = jnp.maximum(m, jnp.max(s, axis=-1, keepdims=True))
    p_own = jnp.exp(s_own - m)
    l = p_own
    for pgi in range(nblk * ppb):
        sc[pgi] = jnp.exp(sc[pgi] - m)
        l = l + jnp.sum(sc[pgi], axis=-1, keepdims=True)
    inv = 1.0 / l
    return [p * inv for p in sc], p_own * inv


def _dec_gather_pv_kernel(pt_ref, sc_ref, km_ref, q_ref, kn_ref, vn_ref, *refs, pg, nh, nblk):
    del pt_ref
    v_refs, (o_ref, vd_ref, p_ref, pown_ref, acc_ref) = refs[:pg], refs[pg:]
    j = pl.program_id(1)
    page = v_refs[0].shape[0] // nh
    a = q_ref.shape[-1]
    hrows = _head_rows((nh, a))

    @pl.when(j == 0)
    def _():
        ps, p_own = _dec_probs(sc_ref, km_ref, q_ref[...], kn_ref[...], hrows, nblk, MOBA_BLOCK // page)
        for pgi, p in enumerate(ps):
            p_ref[pgi] = p
        pown_ref[...] = jnp.broadcast_to(p_own, pown_ref.shape)
        acc_ref[...] = jnp.zeros_like(acc_ref)

    for p in range(pg):
        vd_ref[p * page:(p + 1) * page, :] = _page_rows_to_dense(v_refs[p], nh).astype(BF16)
    probs = jnp.concatenate([p_ref[j * pg + p] for p in range(pg)], axis=-1).astype(BF16)
    acc_ref[...] += jnp.dot(probs, vd_ref[...], preferred_element_type=F32)

    @pl.when(j == pl.num_programs(1) - 1)
    def _():
        full = acc_ref[...] + pown_ref[:, 0:1] * vn_ref[...]
        o_ref[...] = jnp.sum(jnp.where(hrows, full, 0.0), axis=0, keepdims=True)


def _dec_gather_pv(page_table, scores, kmean, q, k_new, v_new, cache_v, *, pg=8):
    bd, npg = page_table.shape
    n_pool, page, nh, hd = cache_v.shape
    a = nh * hd
    nblk = kmean.shape[1]
    assert npg % pg == 0 and nblk * MOBA_BLOCK == npg * page and nh % SUBLANES == 0
    rows = cache_v.reshape(n_pool, page * nh, hd)
    vspec = lambda p: pl.BlockSpec((None, page * nh, hd), lambda b, j, pt: (pt[b, j * pg + p], 0, 0))
    rowspec = pl.BlockSpec((None, 1, a), lambda b, j, pt: (b, 0, 0))
    grid_spec = pltpu.PrefetchScalarGridSpec(
        num_scalar_prefetch=1, grid=(bd, npg // pg),
        in_specs=[pl.BlockSpec((None, npg, nh, page), lambda b, j, pt: (b, 0, 0, 0)),
                  pl.BlockSpec((None, nblk, a), lambda b, j, pt: (b, 0, 0)),
                  rowspec, rowspec, rowspec] + [vspec(p) for p in range(pg)],
        out_specs=[rowspec, pl.BlockSpec((None, pg * page, a), lambda b, j, pt: (b, j, 0))],
        scratch_shapes=[pltpu.VMEM((npg, nh, page), F32), pltpu.VMEM((nh, LANES), F32),
                        pltpu.VMEM((nh, a), F32)],
    )
    return pl.pallas_call(
        functools.partial(_dec_gather_pv_kernel, pg=pg, nh=nh, nblk=nblk),
        grid_spec=grid_spec,
        out_shape=[jax.ShapeDtypeStruct((bd, 1, a), F32), jax.ShapeDtypeStruct((bd, npg * page, a), BF16)],
        compiler_params=_cparams("parallel", "arbitrary"), name="gather_pv_sample",
    )(page_table, scores, kmean, q, k_new, v_new, *([rows] * pg))


def _dec_pv_dense_kernel(sc_ref, km_ref, q_ref, kn_ref, vn_ref, vd_ref, o_ref, *, nh, nblk, page):
    a = q_ref.shape[-1]
    hrows = _head_rows((nh, a))
    ps, p_own = _dec_probs(sc_ref, km_ref, q_ref[...], kn_ref[...], hrows, nblk, MOBA_BLOCK // page)
    probs = jnp.concatenate(ps, axis=-1).astype(BF16)
    full = jnp.dot(probs, vd_ref[...], preferred_element_type=F32) + p_own * vn_ref[...]
    o_ref[...] = jnp.sum(jnp.where(hrows, full, 0.0), axis=0, keepdims=True)


def _dec_pv_dense(scores, kmean, q, k_new, v_new, vd):
    bd, npg, nh, page = scores.shape
    _, lk, a = vd.shape
    nblk = kmean.shape[1]
    rowspec = pl.BlockSpec((None, 1, a), lambda b: (b, 0, 0))
    return pl.pallas_call(
        functools.partial(_dec_pv_dense_kernel, nh=nh, nblk=nblk, page=page),
        grid=(bd,),
        in_specs=[pl.BlockSpec((None, npg, nh, page), lambda b: (b, 0, 0, 0)),
                  pl.BlockSpec((None, nblk, a), lambda b: (b, 0, 0)),
                  rowspec, rowspec, rowspec, pl.BlockSpec((None, lk, a), lambda b: (b, 0, 0))],
        out_specs=rowspec,
        out_shape=jax.ShapeDtypeStruct((bd, 1, a), F32),
        compiler_params=_cparams("parallel"), name="pv_dense_sample",
    )(scores, kmean, q, k_new, v_new, vd)


def kernel(x_prompt, x_sample, state_conv, cache_k, cache_v, page_table, g_a, w_in_a, w_dw, b_dw, ln_g_a, ln_b_a, w_out_a, g_kv, w_k, w_v, g_b, w_in_b, w_out_b, g_final):
    n_a, n_b = w_in_a.shape[0], w_in_b.shape[0]
    bp, lp, d = x_prompt.shape
    bd, ld, _ = x_sample.shape
    n_pool, page, nh, hd = cache_k.shape
    a = nh * hd
    nhist = CONV_W - 1
    assert ld == 1 and hd == HEAD_DIM and lp >= nhist
    assert (page_table.shape[1] * page) % MOBA_BLOCK == 0

    w_in_a_h, w_out_a_h = w_in_a.astype(BF16), w_out_a.astype(BF16)
    w_k_h, w_v_h, w_vt_h = w_k.astype(BF16), w_v.astype(BF16), w_v.T.astype(BF16)
    w_in_b_h, w_out_b_h = w_in_b.astype(BF16), w_out_b.astype(BF16)

    x = x_prompt
    bufs = []
    for i in range(n_a):
        x, buf = _a_layer_prompt(x, g_a[i], w_in_a_h[i], w_dw[i], b_dw[i], ln_g_a[i], ln_b_a[i], w_out_a_h[i])
        bufs.append(buf[:, HALO - nhist:, :])
    k_p, v_p, kb, vt, km = _kv_proj(x, g_kv, w_k_h, w_v_h, w_vt_h)
    km = km.reshape(bp, lp // MOBA_BLOCK, a)
    for j in range(n_b):
        q, qs, sz = _qz_proj(x, g_b[j], w_in_b_h[j], tm=512)
        o = _moba_prompt(q, qs, kb, vt, km)
        x = _out_proj(o, sz, x, w_out_b_h[j], g_final, final=(j == n_b - 1), tm=512)
    y_prompt = x
    conv_prompt = jnp.stack(bufs)

    xs = x_sample.reshape(bd, d)
    new_states = []
    for i in range(n_a):
        glu, sz = _in_glu(xs, g_a[i], w_in_a_h[i])
        y, ns = _dec_conv(state_conv[i], glu, sz, w_dw[i], b_dw[i], ln_g_a[i], ln_b_a[i])
        xs = _res_proj(y, xs, w_out_a_h[i])
        new_states.append(ns)
    k_s, v_s = _kv_proj(xs.reshape(1, bd, d), g_kv, w_k_h, w_v_h)
    k_row, v_row = k_s.reshape(bd, 1, a), v_s.reshape(bd, 1, a)
    xs3 = xs.reshape(1, bd, d)
    kmean = kd = vd = None
    for j in range(n_b):
        q, _, sz = _qz_proj(xs3, g_b[j], w_in_b_h[j], tm=bd)
        q_row = q.reshape(bd, 1, a)
        if j == 0:
            scores, kmean, kd = _dec_gather_scores(page_table, q_row, cache_k)
            kmean = kmean.reshape(bd, -1, a)
            o, vd = _dec_gather_pv(page_table, scores, kmean, q_row, k_row, v_row, cache_v)
        else:
            scores = _dec_scores_dense(q_row, kd, nh=nh, page=page)
            o = _dec_pv_dense(scores, kmean, q_row, k_row, v_row, vd)
        xs3 = _out_proj(o.reshape(1, bd, a), sz, xs3, w_out_b_h[j], g_final, final=(j == n_b - 1), tm=bd)
    y_sample = xs3.reshape(bd, 1, d)

    return (y_prompt, y_sample,
            k_p.reshape(bp, lp, nh, hd), v_p.reshape(bp, lp, nh, hd), conv_prompt,
            k_s.reshape(bd, 1, nh, hd), v_s.reshape(bd, 1, nh, hd), jnp.stack(new_states))
```

```python
import functools

import jax
import jax.numpy as jnp
from jax import lax
from jax.experimental import pallas as pl
from jax.experimental.pallas import tpu as pltpu

F32 = jnp.float32
BF16 = jnp.bfloat16

EPS = 1e-6
CONV_W = 31
HALO = 32
HEAD_DIM = 64
VT_ROWS = HEAD_DIM + 16
MOBA_BLOCK = 256
MOBA_TOPK = 3
LANES = 128
SUBLANES = 8
VMEM_LIMIT = 56 * 1024 * 1024
NEG_INF = float("-inf")

_NT = (((1,), (1,)), ((), ()))


def _cparams(*sem):
    return pltpu.CompilerParams(dimension_semantics=sem, vmem_limit_bytes=VMEM_LIMIT)


def _rms(x, g):
    return x * lax.rsqrt(jnp.mean(x * x, axis=-1, keepdims=True) + EPS) * g


def _silu(x):
    return x * jax.nn.sigmoid(x)


def _full(shape):
    return pl.BlockSpec(shape, lambda *_: (0,) * len(shape))


def _conv_tap_groups():
    groups = {}
    for j in range(CONV_W):
        off = HALO - (CONV_W - 1) + j
        groups.setdefault(off % SUBLANES, []).append((j, off - off % SUBLANES))
    return groups


def _a_layer_kernel(x_ref, g_ref, win_ref, wdw_ref, bdw_ref, lng_ref, lnb_ref, wout_ref,
                    o_ref, buf_ref, s_ref, sz_ref, xn_ref, y_ref, wb_ref, *, tm, e, rc, ch):
    t = pl.program_id(1)

    @pl.when((pl.program_id(0) == 0) & (t == 0))
    def _():
        for j in range(CONV_W):
            wb_ref[j] = jnp.broadcast_to(wdw_ref[j:j + 1, :], (SUBLANES, e))

    @pl.when(t == 0)
    def _():
        s_ref[0:HALO, :] = jnp.zeros((HALO, e), F32)

    xn_ref[...] = _rms(x_ref[...], g_ref[...]).astype(BF16)
    for c in range(e // ch):
        xn = xn_ref[...]
        a = jnp.dot(xn, win_ref[:, c * ch:(c + 1) * ch], preferred_element_type=F32)
        gb = jnp.dot(xn, win_ref[:, e + c * ch:e + (c + 1) * ch], preferred_element_type=F32)
        z = jnp.dot(xn, win_ref[:, 2 * e + c * ch:2 * e + (c + 1) * ch], preferred_element_type=F32)
        s_ref[HALO:HALO + tm, c * ch:(c + 1) * ch] = a * jax.nn.sigmoid(gb)
        sz_ref[:, c * ch:(c + 1) * ch] = _silu(z)

    @pl.when(t == pl.num_programs(1) - 1)
    def _():
        buf_ref[...] = s_ref[tm:tm + HALO, :]

    groups = _conv_tap_groups()
    ncb = e // LANES
    nv = rc // SUBLANES

    def row_chunk(ci, _):
        r0 = pl.multiple_of(ci * rc, rc)

        def conv_cb(cb, s1):
            c0 = pl.multiple_of(cb * LANES, LANES)
            w3 = s_ref[pl.ds(r0, rc + HALO), pl.ds(c0, LANES)].reshape(nv + HALO // SUBLANES, SUBLANES, LANES)
            acc = jnp.broadcast_to(bdw_ref[:, pl.ds(c0, LANES)], (rc, LANES))
            for sh, taps in sorted(groups.items()):
                n = nv if sh == 0 else nv + 1
                u = None
                for j, a0 in taps:
                    v0 = a0 // SUBLANES
                    term = wb_ref[j, :, pl.ds(c0, LANES)][None] * w3[v0:v0 + n]
                    u = term if u is None else u + term
                u = u.reshape(n * SUBLANES, LANES)
                acc = acc + (u if sh == 0 else u[sh:sh + rc])
            s_ref[pl.ds(r0, rc), pl.ds(c0, LANES)] = acc
            return s1 + acc

        s1 = lax.fori_loop(0, ncb, conv_cb, jnp.zeros((rc, LANES), F32))
        mu = jnp.sum(s1, axis=-1, keepdims=True) * (1.0 / e)

        def var_cb(cb, s2):
            c0 = pl.multiple_of(cb * LANES, LANES)
            d = s_ref[pl.ds(r0, rc), pl.ds(c0, LANES)] - mu
            return s2 + d * d

        s2 = lax.fori_loop(0, ncb, var_cb, jnp.zeros((rc, LANES), F32))
        rstd = lax.rsqrt(jnp.sum(s2, axis=-1, keepdims=True) * (1.0 / e) + EPS)

        def norm_cb(cb, _):
            c0 = pl.multiple_of(cb * LANES, LANES)
            cv = s_ref[pl.ds(r0, rc), pl.ds(c0, LANES)]
            yn = (cv - mu) * rstd * lng_ref[:, pl.ds(c0, LANES)] + lnb_ref[:, pl.ds(c0, LANES)]
            y = _silu(yn) * sz_ref[pl.ds(r0, rc), pl.ds(c0, LANES)]
            y_ref[pl.ds(r0, rc), pl.ds(c0, LANES)] = y.astype(BF16)
            return 0

        lax.fori_loop(0, ncb, norm_cb, 0)
        return 0

    lax.fori_loop(0, tm // rc, row_chunk, 0)
    s_ref[0:HALO, :] = s_ref[tm:tm + HALO, :]
    o_ref[...] = x_ref[...] + jnp.dot(y_ref[...], wout_ref[...], preferred_element_type=F32)


def _a_layer_prompt(x, g, w_in, w_dw, b_dw, ln_g, ln_b, w_out, *, tm=256, rc=64, ch=512):
    b, l, d = x.shape
    e = w_dw.shape[-1]
    assert l % tm == 0 and tm % rc == 0 and tm >= HALO and e % ch == 0
    kern = functools.partial(_a_layer_kernel, tm=tm, e=e, rc=rc, ch=ch)
    row = lambda w: w.reshape(1, -1)
    return pl.pallas_call(
        kern,
        grid=(b, l // tm),
        in_specs=[
            pl.BlockSpec((None, tm, d), lambda i, t: (i, t, 0)),
            _full((1, d)), _full((d, 3 * e)), _full((CONV_W, e)),
            _full((1, e)), _full((1, e)), _full((1, e)), _full((e, d)),
        ],
        out_specs=[
            pl.BlockSpec((None, tm, d), lambda i, t: (i, t, 0)),
            pl.BlockSpec((None, HALO, e), lambda i, t: (i, 0, 0)),
        ],
        out_shape=[jax.ShapeDtypeStruct((b, l, d), F32), jax.ShapeDtypeStruct((b, HALO, e), F32)],
        scratch_shapes=[
            pltpu.VMEM((HALO + tm, e), F32), pltpu.VMEM((tm, e), F32),
            pltpu.VMEM((tm, d), BF16), pltpu.VMEM((tm, e), BF16),
            pltpu.VMEM((CONV_W, SUBLANES, e), F32),
        ],
        compiler_params=_cparams("arbitrary", "arbitrary"),
        name="a_layer_prompt",
    )(x, row(g), w_in, w_dw, row(b_dw), row(ln_g), row(ln_b), w_out)


def _kv_kernel(*refs, extras):
    if extras:
        x_ref, g_ref, wk_ref, wv_ref, wvt_ref, k_ref, v_ref, kb_ref, vt_ref, km_ref = refs
    else:
        x_ref, g_ref, wk_ref, wv_ref, k_ref, v_ref = refs
    hk = _rms(x_ref[...], g_ref[...]).astype(BF16)
    k = jnp.dot(hk, wk_ref[...], preferred_element_type=F32)
    k_ref[...] = k
    v_ref[...] = jnp.dot(hk, wv_ref[...], preferred_element_type=F32)
    if extras:
        kb_ref[...] = k.astype(BF16)
        nh = vt_ref.shape[0]
        vt = lax.dot_general(wvt_ref[...], hk, _NT, preferred_element_type=F32)
        vt_ref[:, 0:HEAD_DIM, :] = vt.reshape(nh, HEAD_DIM, vt.shape[-1]).astype(BF16)
        vt_ref[:, HEAD_DIM:, :] = jnp.ones((nh, VT_ROWS - HEAD_DIM, vt.shape[-1]), BF16)
        km_ref[...] = jnp.sum(k, axis=0, keepdims=True) * (1.0 / MOBA_BLOCK)


def _kv_proj(x, g, w_k, w_v, w_vt=None):
    b, l, d = x.shape
    a = w_k.shape[1]
    extras = w_vt is not None
    tm = MOBA_BLOCK if extras else l
    assert l % tm == 0
    xspec = pl.BlockSpec((None, tm, d), lambda i, t: (i, t, 0))
    ospec = pl.BlockSpec((None, tm, a), lambda i, t: (i, t, 0))
    in_specs = [xspec, _full((1, d)), _full((d, a)), _full((d, a))]
    out_specs = [ospec, ospec]
    out_shape = [jax.ShapeDtypeStruct((b, l, a), F32)] * 2
    args = [x, g.reshape(1, d), w_k, w_v]
    if extras:
        in_specs.append(_full((a, d)))
        args.append(w_vt)
        nh = a // HEAD_DIM
        out_specs += [ospec, pl.BlockSpec((None, nh, VT_ROWS, tm), lambda i, t: (i, 0, 0, t)),
                      pl.BlockSpec((None, None, 1, a), lambda i, t: (i, t, 0, 0))]
        out_shape += [jax.ShapeDtypeStruct((b, l, a), BF16), jax.ShapeDtypeStruct((b, nh, VT_ROWS, l), BF16),
                      jax.ShapeDtypeStruct((b, l // tm, 1, a), F32)]
    return pl.pallas_call(
        functools.partial(_kv_kernel, extras=extras),
        grid=(b, l // tm), in_specs=in_specs, out_specs=out_specs, out_shape=out_shape,
        compiler_params=_cparams("parallel", "parallel"), name="kv_proj",
    )(*args)


def _qz_kernel(x_ref, g_ref, w_ref, q_ref, qs_ref, sz_ref, *, a):
    xn = _rms(x_ref[...], g_ref[...]).astype(BF16)
    q = jnp.dot(xn, w_ref[:, 0:a], preferred_element_type=F32)
    z = jnp.dot(xn, w_ref[:, a:2 * a], preferred_element_type=F32)
    q_ref[...] = q
    qs_ref[...] = (q * (HEAD_DIM ** -0.5)).astype(BF16)
    sz_ref[...] = _silu(z)


def _qz_proj(x, g, w_in, *, tm):
    b, l, d = x.shape
    a = w_in.shape[1] // 2
    assert l % tm == 0
    xspec = pl.BlockSpec((None, tm, d), lambda i, t: (i, t, 0))
    ospec = pl.BlockSpec((None, tm, a), lambda i, t: (i, t, 0))
    return pl.pallas_call(
        functools.partial(_qz_kernel, a=a),
        grid=(b, l // tm),
        in_specs=[xspec, _full((1, d)), _full((d, 2 * a))],
        out_specs=[ospec, ospec, ospec],
        out_shape=[jax.ShapeDtypeStruct((b, l, a), F32), jax.ShapeDtypeStruct((b, l, a), BF16),
                   jax.ShapeDtypeStruct((b, l, a), F32)],
        compiler_params=_cparams("parallel", "parallel"), name="qz_proj",
    )(x, g.reshape(1, d), w_in)


def _out_kernel(o_ref, sz_ref, x_ref, w_ref, gf_ref, y_ref, *, final):
    y = (o_ref[...] * sz_ref[...]).astype(BF16)
    xo = x_ref[...] + jnp.dot(y, w_ref[...], preferred_element_type=F32)
    y_ref[...] = _rms(xo, gf_ref[...]) if final else xo


def _out_proj(o, sz, x, w_out, g_final, *, final, tm):
    b, l, d = x.shape
    a = o.shape[-1]
    assert l % tm == 0
    aspec = pl.BlockSpec((None, tm, a), lambda i, t: (i, t, 0))
    xspec = pl.BlockSpec((None, tm, d), lambda i, t: (i, t, 0))
    return pl.pallas_call(
        functools.partial(_out_kernel, final=final),
        grid=(b, l // tm),
        in_specs=[aspec, aspec, xspec, _full((a, d)), _full((1, d))],
        out_specs=xspec,
        out_shape=jax.ShapeDtypeStruct((b, l, d), F32),
        compiler_params=_cparams("parallel", "parallel"), name="out_proj",
    )(o, sz, x, w_out, g_final.reshape(1, d))


def _moba_prompt_kernel(q_ref, qs_ref, k_ref, vt_ref, km_ref, o_ref, sel_ref, qh_ref, acc_ref, *, nblk, sb, nh):
    i = pl.program_id(2)
    blk = MOBA_BLOCK
    w = sb * blk
    lane = lax.broadcasted_iota(jnp.int32, (1, LANES), 1)
    n_iota = lax.broadcasted_iota(jnp.int32, (nblk, blk), 0)
    pair = lambda hh: slice((hh // 2) * LANES, (hh // 2 + 1) * LANES)
    for hh in range(nh):
        hmask = (lane >> 6) == hh % 2
        g = lax.dot_general(km_ref[:, pair(hh)], jnp.where(hmask, q_ref[:, pair(hh)], 0.0), _NT,
                            precision=lax.Precision.HIGHEST, preferred_element_type=F32)
        g = jnp.where(n_iota < i, g, NEG_INF)
        sel = jnp.zeros((nblk, blk), F32)
        for _ in range(MOBA_TOPK):
            m = jnp.max(g, axis=0, keepdims=True)
            idx = jnp.min(jnp.where(g == m, n_iota, nblk), axis=0, keepdims=True)
            hit = n_iota == idx
            sel = jnp.where(hit & (m > NEG_INF), 1.0, sel)
            g = jnp.where(hit, NEG_INF, g)
        sel_ref[hh] = sel
        qs = qs_ref[:, pair(hh)]
        qh_ref[hh] = jnp.where(hmask, qs, jnp.zeros_like(qs))

    def scores(start, rows):
        return [lax.dot_general(k_ref[pl.ds(start, rows), pair(hh)], qh_ref[hh], _NT, preferred_element_type=F32)
                for hh in range(nh)]

    k_iota = lax.broadcasted_iota(jnp.int32, (blk, blk), 0)
    q_iota = lax.broadcasted_iota(jnp.int32, (blk, blk), 1)
    own = pl.multiple_of(i * blk, blk)
    sts = scores(own, blk)
    maxes = []
    for hh in range(nh):
        st = jnp.where(k_iota <= q_iota, sts[hh], NEG_INF)
        m = jnp.max(st, axis=0, keepdims=True)
        maxes.append(m)
        acc_ref[hh] = jnp.dot(vt_ref[hh, :, pl.ds(own, blk)], jnp.exp(st - m).astype(BF16),
                              preferred_element_type=F32)

    def body(t, maxes):
        start = pl.multiple_of(t * w, w)
        sts = scores(start, w)
        out = []
        for hh in range(nh):
            m = maxes[hh]
            st = sts[hh]
            sels = [sel_ref[hh, pl.ds(t * sb + j, 1), :] > 0.0 for j in range(sb)]
            m_new = m
            for j in range(sb):
                mj = jnp.max(st[j * blk:(j + 1) * blk], axis=0, keepdims=True)
                m_new = jnp.maximum(m_new, jnp.where(sels[j], mj, NEG_INF))
            ps = [jnp.where(sels[j], jnp.exp(st[j * blk:(j + 1) * blk] - m_new), 0.0).astype(BF16)
                  for j in range(sb)]
            acc_ref[hh] = jnp.exp(m - m_new) * acc_ref[hh] + jnp.dot(
                vt_ref[hh, :, pl.ds(start, w)], jnp.concatenate(ps, axis=0), preferred_element_type=F32)
            out.append(m_new)
        return tuple(out)

    lax.fori_loop(0, (i + sb - 1) // sb, body, tuple(maxes))
    ot = jnp.concatenate([acc_ref[hh, 0:HEAD_DIM, :] / acc_ref[hh, HEAD_DIM:HEAD_DIM + 1, :] for hh in range(nh)],
                         axis=0)
    o_ref[...] = ot.T


def _moba_prompt(q, qs, kb, vt, km, *, sb=2, nh=8):
    b, l, a = q.shape
    blk = MOBA_BLOCK
    nblk = l // blk
    wd = nh * HEAD_DIM
    assert l % blk == 0 and nblk % sb == 0 and a % wd == 0 and wd % LANES == 0 and LANES == 2 * HEAD_DIM
    qspec = pl.BlockSpec((None, blk, wd), lambda i, h, t: (i, t, h))
    return pl.pallas_call(
        functools.partial(_moba_prompt_kernel, nblk=nblk, sb=sb, nh=nh),
        grid=(b, a // wd, nblk),
        in_specs=[
            qspec, qspec,
            pl.BlockSpec((None, l, wd), lambda i, h, t: (i, 0, h)),
            pl.BlockSpec((None, nh, VT_ROWS, l), lambda i, h, t: (i, h, 0, 0)),
            pl.BlockSpec((None, nblk, wd), lambda i, h, t: (i, 0, h)),
        ],
        out_specs=qspec,
        out_shape=jax.ShapeDtypeStruct((b, l, a), F32),
        scratch_shapes=[
            pltpu.VMEM((nh, nblk, blk), F32), pltpu.VMEM((nh, blk, LANES), BF16),
            pltpu.VMEM((nh, VT_ROWS, blk), F32),
        ],
        compiler_params=_cparams("parallel", "parallel", "arbitrary"),
        name="moba_prompt",
    )(q, qs, kb, vt, km)


def _in_glu_kernel(x_ref, g_ref, wa_ref, wg_ref, wz_ref, glu_ref, sz_ref):
    xn = _rms(x_ref[...], g_ref[...]).astype(BF16)
    a = jnp.dot(xn, wa_ref[...], preferred_element_type=F32)
    gb = jnp.dot(xn, wg_ref[...], preferred_element_type=F32)
    z = jnp.dot(xn, wz_ref[...], preferred_element_type=F32)
    glu_ref[...] = a * jax.nn.sigmoid(gb)
    sz_ref[...] = _silu(z)


def _in_glu(x, g, w_in, *, ch=512):
    n, d = x.shape
    e = w_in.shape[1] // 3
    nc = e // ch
    wspec = lambda k: pl.BlockSpec((d, ch), lambda c: (0, c + k * nc))
    ospec = pl.BlockSpec((n, ch), lambda c: (0, c))
    return pl.pallas_call(
        _in_glu_kernel,
        grid=(nc,),
        in_specs=[_full((n, d)), _full((1, d)), wspec(0), wspec(1), wspec(2)],
        out_specs=[ospec, ospec],
        out_shape=[jax.ShapeDtypeStruct((n, e), F32)] * 2,
        compiler_params=_cparams("parallel"), name="in_glu_sample",
    )(x, g.reshape(1, d), w_in, w_in, w_in)


def _dec_conv_kernel(st_ref, glu_ref, sz_ref, wdw_ref, bdw_ref, lng_ref, lnb_ref,
                     y_ref, ns_ref, conv_ref, *, bt):
    nh = CONV_W - 1
    w_hist = wdw_ref[0:nh, :]
    for i in range(bt):
        conv_ref[i:i + 1, :] = jnp.sum(st_ref[i] * w_hist, axis=0, keepdims=True)
        ns_ref[i, 0:nh - 1, :] = st_ref[i, 1:nh, :]
        ns_ref[i, nh - 1:nh, :] = glu_ref[i:i + 1, :]
    conv = conv_ref[...] + glu_ref[...] * wdw_ref[nh:nh + 1, :] + bdw_ref[...]
    mu = jnp.mean(conv, axis=-1, keepdims=True)
    d = conv - mu
    yn = d * lax.rsqrt(jnp.mean(d * d, axis=-1, keepdims=True) + EPS) * lng_ref[...] + lnb_ref[...]
    y_ref[...] = (_silu(yn) * sz_ref[...]).astype(BF16)


def _dec_conv(state, glu, sz, w_dw, b_dw, ln_g, ln_b, *, bt=16):
    n, nh, e = state.shape
    assert n % bt == 0 and nh == CONV_W - 1
    rspec = pl.BlockSpec((bt, e), lambda i: (i, 0))
    sspec = pl.BlockSpec((bt, nh, e), lambda i: (i, 0, 0))
    row = lambda w: w.reshape(1, e)
    return pl.pallas_call(
        functools.partial(_dec_conv_kernel, bt=bt),
        grid=(n // bt,),
        in_specs=[sspec, rspec, rspec, _full((CONV_W, e)), _full((1, e)), _full((1, e)), _full((1, e))],
        out_specs=[rspec, sspec],
        out_shape=[jax.ShapeDtypeStruct((n, e), BF16), jax.ShapeDtypeStruct((n, nh, e), F32)],
        scratch_shapes=[pltpu.VMEM((bt, e), F32)],
        compiler_params=_cparams("parallel"), name="conv_sample",
    )(state, glu, sz, w_dw, row(b_dw), row(ln_g), row(ln_b))


def _res_proj_kernel(y_ref, x_ref, w_ref, o_ref):
    o_ref[...] = x_ref[...] + jnp.dot(y_ref[...], w_ref[...], preferred_element_type=F32)


def _res_proj(y, x, w):
    n, d = x.shape
    e = y.shape[1]
    return pl.pallas_call(
        _res_proj_kernel, grid=(1,),
        in_specs=[_full((n, e)), _full((n, d)), _full((e, d))],
        out_specs=_full((n, d)), out_shape=jax.ShapeDtypeStruct((n, d), F32),
        compiler_params=_cparams("arbitrary"), name="res_proj_sample",
    )(y, x, w)


def _head_rows(shape):
    r = lax.broadcasted_iota(jnp.int32, shape, 0)
    c = lax.broadcasted_iota(jnp.int32, shape, 1)
    return r == (c >> 6)


def _dec_scores_kernel(pt_ref, qt_ref, *refs, pg, nh):
    del pt_ref
    k_refs, sc_ref = refs[:pg], refs[pg]
    hd, page = k_refs[0].shape[1:]
    qt = qt_ref[...]
    qcols = [jnp.broadcast_to(qt[:, h:h + 1], (hd, page)) for h in range(nh)]
    row_id = lax.broadcasted_iota(jnp.int32, (nh, page), 0)
    for p in range(pg):
        s = jnp.zeros((nh, page), F32)
        for h in range(nh):
            s = jnp.where(row_id == h, jnp.sum(k_refs[p][h] * qcols[h], axis=0, keepdims=True), s)
        sc_ref[p] = s * (HEAD_DIM ** -0.5)


def _dec_scores(page_table, qt, cache_t, *, pg=8):
    bd, npg = page_table.shape
    _, nh, hd, page = cache_t.shape
    assert npg % pg == 0 and page == LANES and hd % SUBLANES == 0
    kspec = lambda p: pl.BlockSpec((None, nh, hd, page), lambda b, j, pt: (pt[b, j * pg + p], 0, 0, 0))
    grid_spec = pltpu.PrefetchScalarGridSpec(
        num_scalar_prefetch=1, grid=(bd, npg // pg),
        in_specs=[pl.BlockSpec((None, hd, nh), lambda b, j, pt: (b, 0, 0))] + [kspec(p) for p in range(pg)],
        out_specs=pl.BlockSpec((None, pg, nh, page), lambda b, j, pt: (b, j, 0, 0)),
    )
    return pl.pallas_call(
        functools.partial(_dec_scores_kernel, pg=pg, nh=nh),
        grid_spec=grid_spec,
        out_shape=jax.ShapeDtypeStruct((bd, npg, nh, page), F32),
        compiler_params=_cparams("parallel", "arbitrary"), name="scores_sample",
    )(page_table, qt, *([cache_t] * pg))


def _dec_probs(sc_ref, q, kn, hrows, nblk, ppb):
    nh = hrows.shape[0]
    g = [functools.reduce(jnp.add, [jnp.sum(sc_ref[n * ppb + r], axis=-1, keepdims=True) for r in range(ppb)])
         for n in range(nblk)]
    sel = [jnp.zeros((nh, 1), F32) for _ in range(nblk)]
    for _ in range(min(MOBA_TOPK, nblk)):
        m = functools.reduce(jnp.maximum, g)
        idx = functools.reduce(jnp.minimum, [jnp.where(g[n] == m, n, nblk) for n in range(nblk)])
        for n in range(nblk):
            hit = idx == n
            sel[n] = jnp.where(hit, 1.0, sel[n])
            g[n] = jnp.where(hit, NEG_INF, g[n])
    s_own = jnp.sum(jnp.where(hrows, kn * q, 0.0), axis=-1, keepdims=True) * (HEAD_DIM ** -0.5)
    m = s_own
    sc = []
    for pgi in range(nblk * ppb):
        s = jnp.where(sel[pgi // ppb] > 0.0, sc_ref[pgi], NEG_INF)
        sc.append(s)
        m = jnp.maximum(m, jnp.max(s, axis=-1, keepdims=True))
    p_own = jnp.exp(s_own - m)
    l = p_own
    for pgi in range(nblk * ppb):
        sc[pgi] = jnp.exp(sc[pgi] - m)
        l = l + jnp.sum(sc[pgi], axis=-1, keepdims=True)
    inv = 1.0 / l
    return [p * inv for p in sc], p_own * inv


def _dec_pv_kernel(pt_ref, sc_ref, q_ref, kn_ref, vnt_ref, *refs, pg, nh):
    del pt_ref
    v_refs, (o_ref, p_ref, pown_ref, acc_ref) = refs[:pg], refs[pg:]
    j = pl.program_id(1)
    hd, page = v_refs[0].shape[1:]
    a = q_ref.shape[-1]

    @pl.when(j == 0)
    def _():
        nblk = sc_ref.shape[0] * page // MOBA_BLOCK
        ps, p_own = _dec_probs(sc_ref, q_ref[...], kn_ref[...], _head_rows((nh, a)), nblk, MOBA_BLOCK // page)
        for pgi, p in enumerate(ps):
            p_ref[pgi] = p
        pown_ref[...] = jnp.broadcast_to(p_own, pown_ref.shape)
        acc_ref[...] = jnp.zeros_like(acc_ref)

    for h in range(nh):
        acc = acc_ref[h]
        for p in range(pg):
            acc = acc + v_refs[p][h] * p_ref[j * pg + p, h:h + 1, :]
        acc_ref[h] = acc

    @pl.when(j == pl.num_programs(1) - 1)
    def _():
        lane_id = lax.broadcasted_iota(jnp.int32, (hd, nh), 1)
        out = jnp.zeros((hd, nh), F32)
        for h in range(nh):
            col = jnp.sum(acc_ref[h], axis=-1, keepdims=True) + pown_ref[h:h + 1, 0:1] * vnt_ref[:, h:h + 1]
            out = jnp.where(lane_id == h, col, out)
        o_ref[...] = out


def _dec_pv(page_table, scores, q, k_new, v_new_t, cache_t, *, pg=8):
    bd, npg = page_table.shape
    _, nh, hd, page = cache_t.shape
    a = nh * hd
    assert npg % pg == 0 and (npg * page) % MOBA_BLOCK == 0 and MOBA_BLOCK % page == 0
    vspec = lambda p: pl.BlockSpec((None, nh, hd, page), lambda b, j, pt: (pt[b, j * pg + p], 0, 0, 0))
    rowspec = pl.BlockSpec((None, 1, a), lambda b, j, pt: (b, 0, 0))
    colspec = pl.BlockSpec((None, hd, nh), lambda b, j, pt: (b, 0, 0))
    grid_spec = pltpu.PrefetchScalarGridSpec(
        num_scalar_prefetch=1, grid=(bd, npg // pg),
        in_specs=[pl.BlockSpec((None, npg, nh, page), lambda b, j, pt: (b, 0, 0, 0)),
                  rowspec, rowspec, colspec] + [vspec(p) for p in range(pg)],
        out_specs=colspec,
        scratch_shapes=[pltpu.VMEM((npg, nh, page), F32), pltpu.VMEM((nh, LANES), F32),
                        pltpu.VMEM((nh, hd, page), F32)],
    )
    return pl.pallas_call(
        functools.partial(_dec_pv_kernel, pg=pg, nh=nh),
        grid_spec=grid_spec,
        out_shape=jax.ShapeDtypeStruct((bd, hd, nh), F32),
        compiler_params=_cparams("parallel", "arbitrary"), name="pv_sample",
    )(page_table, scores, q, k_new, v_new_t, *([cache_t] * pg))


def kernel(x_prompt, x_sample, state_conv, cache_k, cache_v, page_table, g_a, w_in_a, w_dw, b_dw, ln_g_a, ln_b_a, w_out_a, g_kv, w_k, w_v, g_b, w_in_b, w_out_b, g_final):
    n_a, n_b = w_in_a.shape[0], w_in_b.shape[0]
    bp, lp, d = x_prompt.shape
    bd, ld, _ = x_sample.shape
    n_pool, page, nh, hd = cache_k.shape
    a = nh * hd
    nhist = CONV_W - 1
    assert ld == 1 and hd == HEAD_DIM and lp >= nhist
    assert (page_table.shape[1] * page) % MOBA_BLOCK == 0

    w_in_a_h, w_out_a_h = w_in_a.astype(BF16), w_out_a.astype(BF16)
    w_k_h, w_v_h, w_vt_h = w_k.astype(BF16), w_v.astype(BF16), w_v.T.astype(BF16)
    w_in_b_h, w_out_b_h = w_in_b.astype(BF16), w_out_b.astype(BF16)

    x = x_prompt
    bufs = []
    for i in range(n_a):
        x, buf = _a_layer_prompt(x, g_a[i], w_in_a_h[i], w_dw[i], b_dw[i], ln_g_a[i], ln_b_a[i], w_out_a_h[i])
        bufs.append(buf[:, HALO - nhist:, :])
    k_p, v_p, kb, vt, km = _kv_proj(x, g_kv, w_k_h, w_v_h, w_vt_h)
    km = km.reshape(bp, lp // MOBA_BLOCK, a)
    for j in range(n_b):
        q, qs, sz = _qz_proj(x, g_b[j], w_in_b_h[j], tm=512)
        o = _moba_prompt(q, qs, kb, vt, km)
        x = _out_proj(o, sz, x, w_out_b_h[j], g_final, final=(j == n_b - 1), tm=512)
    y_prompt = x
    conv_prompt = jnp.stack(bufs)

    xs = x_sample.reshape(bd, d)
    new_states = []
    for i in range(n_a):
        glu, sz = _in_glu(xs, g_a[i], w_in_a_h[i])
        y, ns = _dec_conv(state_conv[i], glu, sz, w_dw[i], b_dw[i], ln_g_a[i], ln_b_a[i])
        xs = _res_proj(y, xs, w_out_a_h[i])
        new_states.append(ns)
    k_s, v_s = _kv_proj(xs.reshape(1, bd, d), g_kv, w_k_h, w_v_h)
    ck_t = jnp.transpose(cache_k, (0, 2, 3, 1))
    cv_t = jnp.transpose(cache_v, (0, 2, 3, 1))
    heads_last = lambda r: jnp.transpose(r.reshape(bd, nh, hd), (0, 2, 1))
    k_row, v_new_t = k_s.reshape(bd, 1, a), heads_last(v_s)
    xs3 = xs.reshape(1, bd, d)
    for j in range(n_b):
        q, _, sz = _qz_proj(xs3, g_b[j], w_in_b_h[j], tm=bd)
        scores = _dec_scores(page_table, heads_last(q), ck_t)
        o_t = _dec_pv(page_table, scores, q.reshape(bd, 1, a), k_row, v_new_t, cv_t)
        o = jnp.transpose(o_t, (0, 2, 1)).reshape(1, bd, a)
        xs3 = _out_proj(o, sz, xs3, w_out_b_h[j], g_final, final=(j == n_b - 1), tm=bd)
    y_sample = xs3.reshape(bd, 1, d)

    return (y_prompt, y_sample,
            k_p.reshape(bp, lp, nh, hd), v_p.reshape(bp, lp, nh, hd), conv_prompt,
            k_s.reshape(bd, 1, nh, hd), v_s.reshape(bd, 1, nh, hd), jnp.stack(new_states))
```

```python
import functools

import jax
import jax.numpy as jnp
from jax import lax
from jax.experimental import pallas as pl
from jax.experimental.pallas import tpu as pltpu

F32 = jnp.float32
BF16 = jnp.bfloat16

EPS = 1e-6
CONV_W = 31
HALO = 32
HEAD_DIM = 64
VT_ROWS = HEAD_DIM + 16
MOBA_BLOCK = 256
MOBA_TOPK = 3
LANES = 128
SUBLANES = 8
VMEM_LIMIT = 56 * 1024 * 1024
NEG_INF = float("-inf")

_NT = (((1,), (1,)), ((), ()))


def _cparams(*sem):
    return pltpu.CompilerParams(dimension_semantics=sem, vmem_limit_bytes=VMEM_LIMIT)


def _rms(x, g):
    return x * lax.rsqrt(jnp.mean(x * x, axis=-1, keepdims=True) + EPS) * g


def _silu(x):
    return x * jax.nn.sigmoid(x)


def _full(shape):
    return pl.BlockSpec(shape, lambda *_: (0,) * len(shape))


def _conv_tap_groups():
    groups = {}
    for j in range(CONV_W):
        off = HALO - (CONV_W - 1) + j
        groups.setdefault(off % SUBLANES, []).append((j, off - off % SUBLANES))
    return groups


def _a_layer_kernel(x_ref, g_ref, win_ref, wdw_ref, bdw_ref, lng_ref, lnb_ref, wout_ref,
                    o_ref, buf_ref, s_ref, sz_ref, xn_ref, y_ref, wb_ref, *, tm, e, rc, ch):
    t = pl.program_id(1)

    @pl.when((pl.program_id(0) == 0) & (t == 0))
    def _():
        for j in range(CONV_W):
            wb_ref[j] = jnp.broadcast_to(wdw_ref[j:j + 1, :], (SUBLANES, e))

    @pl.when(t == 0)
    def _():
        s_ref[0:HALO, :] = jnp.zeros((HALO, e), F32)

    xn_ref[...] = _rms(x_ref[...], g_ref[...]).astype(BF16)
    for c in range(e // ch):
        xn = xn_ref[...]
        a = jnp.dot(xn, win_ref[:, c * ch:(c + 1) * ch], preferred_element_type=F32)
        gb = jnp.dot(xn, win_ref[:, e + c * ch:e + (c + 1) * ch], preferred_element_type=F32)
        z = jnp.dot(xn, win_ref[:, 2 * e + c * ch:2 * e + (c + 1) * ch], preferred_element_type=F32)
        s_ref[HALO:HALO + tm, c * ch:(c + 1) * ch] = a * jax.nn.sigmoid(gb)
        sz_ref[:, c * ch:(c + 1) * ch] = _silu(z)

    @pl.when(t == pl.num_programs(1) - 1)
    def _():
        buf_ref[...] = s_ref[tm:tm + HALO, :]

    groups = _conv_tap_groups()
    ncb = e // LANES
    nv = rc // SUBLANES

    def row_chunk(ci, _):
        r0 = pl.multiple_of(ci * rc, rc)

        def conv_cb(cb, s1):
            c0 = pl.multiple_of(cb * LANES, LANES)
            w3 = s_ref[pl.ds(r0, rc + HALO), pl.ds(c0, LANES)].reshape(nv + HALO // SUBLANES, SUBLANES, LANES)
            acc = jnp.broadcast_to(bdw_ref[:, pl.ds(c0, LANES)], (rc, LANES))
            for sh, taps in sorted(groups.items()):
                n = nv if sh == 0 else nv + 1
                u = None
                for j, a0 in taps:
                    v0 = a0 // SUBLANES
                    term = wb_ref[j, :, pl.ds(c0, LANES)][None] * w3[v0:v0 + n]
                    u = term if u is None else u + term
                u = u.reshape(n * SUBLANES, LANES)
                acc = acc + (u if sh == 0 else u[sh:sh + rc])
            s_ref[pl.ds(r0, rc), pl.ds(c0, LANES)] = acc
            return s1 + acc

        s1 = lax.fori_loop(0, ncb, conv_cb, jnp.zeros((rc, LANES), F32))
        mu = jnp.sum(s1, axis=-1, keepdims=True) * (1.0 / e)

        def var_cb(cb, s2):
            c0 = pl.multiple_of(cb * LANES, LANES)
            d = s_ref[pl.ds(r0, rc), pl.ds(c0, LANES)] - mu
            return s2 + d * d

        s2 = lax.fori_loop(0, ncb, var_cb, jnp.zeros((rc, LANES), F32))
        rstd = lax.rsqrt(jnp.sum(s2, axis=-1, keepdims=True) * (1.0 / e) + EPS)

        def norm_cb(cb, _):
            c0 = pl.multiple_of(cb * LANES, LANES)
            cv = s_ref[pl.ds(r0, rc), pl.ds(c0, LANES)]
            yn = (cv - mu) * rstd * lng_ref[:, pl.ds(c0, LANES)] + lnb_ref[:, pl.ds(c0, LANES)]
            y = _silu(yn) * sz_ref[pl.ds(r0, rc), pl.ds(c0, LANES)]
            y_ref[pl.ds(r0, rc), pl.ds(c0, LANES)] = y.astype(BF16)
            return 0

        lax.fori_loop(0, ncb, norm_cb, 0)
        return 0

    lax.fori_loop(0, tm // rc, row_chunk, 0)
    s_ref[0:HALO, :] = s_ref[tm:tm + HALO, :]
    o_ref[...] = x_ref[...] + jnp.dot(y_ref[...], wout_ref[...], preferred_element_type=F32)


def _a_layer_prompt(x, g, w_in, w_dw, b_dw, ln_g, ln_b, w_out, *, tm=256, rc=128, ch=512):
    b, l, d = x.shape
    e = w_dw.shape[-1]
    assert l % tm == 0 and tm % rc == 0 and tm >= HALO and e % ch == 0
    kern = functools.partial(_a_layer_kernel, tm=tm, e=e, rc=rc, ch=ch)
    row = lambda w: w.reshape(1, -1)
    return pl.pallas_call(
        kern,
        grid=(b, l // tm),
        in_specs=[
            pl.BlockSpec((None, tm, d), lambda i, t: (i, t, 0)),
            _full((1, d)), _full((d, 3 * e)), _full((CONV_W, e)),
            _full((1, e)), _full((1, e)), _full((1, e)), _full((e, d)),
        ],
        out_specs=[
            pl.BlockSpec((None, tm, d), lambda i, t: (i, t, 0)),
            pl.BlockSpec((None, HALO, e), lambda i, t: (i, 0, 0)),
        ],
        out_shape=[jax.ShapeDtypeStruct((b, l, d), F32), jax.ShapeDtypeStruct((b, HALO, e), F32)],
        scratch_shapes=[
            pltpu.VMEM((HALO + tm, e), F32), pltpu.VMEM((tm, e), F32),
            pltpu.VMEM((tm, d), BF16), pltpu.VMEM((tm, e), BF16),
            pltpu.VMEM((CONV_W, SUBLANES, e), F32),
        ],
        compiler_params=_cparams("arbitrary", "arbitrary"),
        name="a_layer_prompt",
    )(x, row(g), w_in, w_dw, row(b_dw), row(ln_g), row(ln_b), w_out)


def _kv_kernel(*refs, extras):
    if extras:
        x_ref, g_ref, wk_ref, wv_ref, wvt_ref, k_ref, v_ref, kb_ref, vt_ref, km_ref = refs
    else:
        x_ref, g_ref, wk_ref, wv_ref, k_ref, v_ref = refs
    hk = _rms(x_ref[...], g_ref[...]).astype(BF16)
    k = jnp.dot(hk, wk_ref[...], preferred_element_type=F32)
    k_ref[...] = k
    v_ref[...] = jnp.dot(hk, wv_ref[...], preferred_element_type=F32)
    if extras:
        kb_ref[...] = k.astype(BF16)
        nh = vt_ref.shape[0]
        vt = lax.dot_general(wvt_ref[...], hk, _NT, preferred_element_type=F32)
        vt_ref[:, 0:HEAD_DIM, :] = vt.reshape(nh, HEAD_DIM, vt.shape[-1]).astype(BF16)
        vt_ref[:, HEAD_DIM:, :] = jnp.ones((nh, VT_ROWS - HEAD_DIM, vt.shape[-1]), BF16)
        km_ref[...] = jnp.sum(k, axis=0, keepdims=True) * (1.0 / MOBA_BLOCK)


def _kv_proj(x, g, w_k, w_v, w_vt=None):
    b, l, d = x.shape
    a = w_k.shape[1]
    extras = w_vt is not None
    tm = MOBA_BLOCK if extras else l
    assert l % tm == 0
    xspec = pl.BlockSpec((None, tm, d), lambda i, t: (i, t, 0))
    ospec = pl.BlockSpec((None, tm, a), lambda i, t: (i, t, 0))
    in_specs = [xspec, _full((1, d)), _full((d, a)), _full((d, a))]
    out_specs = [ospec, ospec]
    out_shape = [jax.ShapeDtypeStruct((b, l, a), F32)] * 2
    args = [x, g.reshape(1, d), w_k, w_v]
    if extras:
        in_specs.append(_full((a, d)))
        args.append(w_vt)
        nh = a // HEAD_DIM
        out_specs += [ospec, pl.BlockSpec((None, nh, VT_ROWS, tm), lambda i, t: (i, 0, 0, t)),
                      pl.BlockSpec((None, None, 1, a), lambda i, t: (i, t, 0, 0))]
        out_shape += [jax.ShapeDtypeStruct((b, l, a), BF16), jax.ShapeDtypeStruct((b, nh, VT_ROWS, l), BF16),
                      jax.ShapeDtypeStruct((b, l // tm, 1, a), F32)]
    return pl.pallas_call(
        functools.partial(_kv_kernel, extras=extras),
        grid=(b, l // tm), in_specs=in_specs, out_specs=out_specs, out_shape=out_shape,
        compiler_params=_cparams("parallel", "parallel"), name="kv_proj",
    )(*args)


def _qz_kernel(x_ref, g_ref, w_ref, q_ref, qs_ref, sz_ref, *, a):
    xn = _rms(x_ref[...], g_ref[...]).astype(BF16)
    q = jnp.dot(xn, w_ref[:, 0:a], preferred_element_type=F32)
    z = jnp.dot(xn, w_ref[:, a:2 * a], preferred_element_type=F32)
    q_ref[...] = q
    qs_ref[...] = (q * (HEAD_DIM ** -0.5)).astype(BF16)
    sz_ref[...] = _silu(z)


def _qz_proj(x, g, w_in, *, tm):
    b, l, d = x.shape
    a = w_in.shape[1] // 2
    assert l % tm == 0
    xspec = pl.BlockSpec((None, tm, d), lambda i, t: (i, t, 0))
    ospec = pl.BlockSpec((None, tm, a), lambda i, t: (i, t, 0))
    return pl.pallas_call(
        functools.partial(_qz_kernel, a=a),
        grid=(b, l // tm),
        in_specs=[xspec, _full((1, d)), _full((d, 2 * a))],
        out_specs=[ospec, ospec, ospec],
        out_shape=[jax.ShapeDtypeStruct((b, l, a), F32), jax.ShapeDtypeStruct((b, l, a), BF16),
                   jax.ShapeDtypeStruct((b, l, a), F32)],
        compiler_params=_cparams("parallel", "parallel"), name="qz_proj",
    )(x, g.reshape(1, d), w_in)


def _out_kernel(o_ref, sz_ref, x_ref, w_ref, gf_ref, y_ref, *, final):
    y = (o_ref[...] * sz_ref[...]).astype(BF16)
    xo = x_ref[...] + jnp.dot(y, w_ref[...], preferred_element_type=F32)
    y_ref[...] = _rms(xo, gf_ref[...]) if final else xo


def _out_proj(o, sz, x, w_out, g_final, *, final, tm):
    b, l, d = x.shape
    a = o.shape[-1]
    assert l % tm == 0
    aspec = pl.BlockSpec((None, tm, a), lambda i, t: (i, t, 0))
    xspec = pl.BlockSpec((None, tm, d), lambda i, t: (i, t, 0))
    return pl.pallas_call(
        functools.partial(_out_kernel, final=final),
        grid=(b, l // tm),
        in_specs=[aspec, aspec, xspec, _full((a, d)), _full((1, d))],
        out_specs=xspec,
        out_shape=jax.ShapeDtypeStruct((b, l, d), F32),
        compiler_params=_cparams("parallel", "parallel"), name="out_proj",
    )(o, sz, x, w_out, g_final.reshape(1, d))


def _moba_prompt_kernel(q_ref, qs_ref, k_ref, vt_ref, km_ref, o_ref, sel_ref, qh_ref, acc_ref, *, nblk, sb, nh):
    i = pl.program_id(2)
    blk = MOBA_BLOCK
    w = sb * blk
    lane = lax.broadcasted_iota(jnp.int32, (1, LANES), 1)
    n_iota = lax.broadcasted_iota(jnp.int32, (nblk, blk), 0)
    pair = lambda hh: slice((hh // 2) * LANES, (hh // 2 + 1) * LANES)
    for hh in range(nh):
        hmask = (lane >> 6) == hh % 2
        g = lax.dot_general(km_ref[:, pair(hh)], jnp.where(hmask, q_ref[:, pair(hh)], 0.0), _NT,
                            precision=lax.Precision.HIGHEST, preferred_element_type=F32)
        g = jnp.where(n_iota < i, g, NEG_INF)
        sel = jnp.zeros((nblk, blk), F32)
        for _ in range(MOBA_TOPK):
            m = jnp.max(g, axis=0, keepdims=True)
            idx = jnp.min(jnp.where(g == m, n_iota, nblk), axis=0, keepdims=True)
            hit = n_iota == idx
            sel = jnp.where(hit & (m > NEG_INF), 1.0, sel)
            g = jnp.where(hit, NEG_INF, g)
        sel_ref[hh] = sel
        qs = qs_ref[:, pair(hh)]
        qh_ref[hh] = jnp.where(hmask, qs, jnp.zeros_like(qs))

    def scores(start, rows):
        return [lax.dot_general(k_ref[pl.ds(start, rows), pair(hh)], qh_ref[hh], _NT, preferred_element_type=F32)
                for hh in range(nh)]

    k_iota = lax.broadcasted_iota(jnp.int32, (blk, blk), 0)
    q_iota = lax.broadcasted_iota(jnp.int32, (blk, blk), 1)
    own = pl.multiple_of(i * blk, blk)
    sts = scores(own, blk)
    maxes = []
    for hh in range(nh):
        st = jnp.where(k_iota <= q_iota, sts[hh], NEG_INF)
        m = jnp.max(st, axis=0, keepdims=True)
        maxes.append(m)
        acc_ref[hh] = jnp.dot(vt_ref[hh, :, pl.ds(own, blk)], jnp.exp(st - m).astype(BF16),
                              preferred_element_type=F32)

    def body(t, maxes):
        start = pl.multiple_of(t * w, w)
        sts = scores(start, w)
        out = []
        for hh in range(nh):
            m = maxes[hh]
            st = sts[hh]
            sels = [sel_ref[hh, pl.ds(t * sb + j, 1), :] > 0.0 for j in range(sb)]
            m_new = m
            for j in range(sb):
                mj = jnp.max(st[j * blk:(j + 1) * blk], axis=0, keepdims=True)
                m_new = jnp.maximum(m_new, jnp.where(sels[j], mj, NEG_INF))
            ps = [jnp.where(sels[j], jnp.exp(st[j * blk:(j + 1) * blk] - m_new), 0.0).astype(BF16)
                  for j in range(sb)]
            acc_ref[hh] = jnp.exp(m - m_new) * acc_ref[hh] + jnp.dot(
                vt_ref[hh, :, pl.ds(start, w)], jnp.concatenate(ps, axis=0), preferred_element_type=F32)
            out.append(m_new)
        return tuple(out)

    lax.fori_loop(0, (i + sb - 1) // sb, body, tuple(maxes))
    ot = jnp.concatenate([acc_ref[hh, 0:HEAD_DIM, :] / acc_ref[hh, HEAD_DIM:HEAD_DIM + 1, :] for hh in range(nh)],
                         axis=0)
    o_ref[...] = ot.T


def _moba_prompt(q, qs, kb, vt, km, *, sb=2, nh=8):
    b, l, a = q.shape
    blk = MOBA_BLOCK
    nblk = l // blk
    wd = nh * HEAD_DIM
    assert l % blk == 0 and nblk % sb == 0 and a % wd == 0 and wd % LANES == 0 and LANES == 2 * HEAD_DIM
    qspec = pl.BlockSpec((None, blk, wd), lambda i, h, t: (i, t, h))
    return pl.pallas_call(
        functools.partial(_moba_prompt_kernel, nblk=nblk, sb=sb, nh=nh),
        grid=(b, a // wd, nblk),
        in_specs=[
            qspec, qspec,
            pl.BlockSpec((None, l, wd), lambda i, h, t: (i, 0, h)),
            pl.BlockSpec((None, nh, VT_ROWS, l), lambda i, h, t: (i, h, 0, 0)),
            pl.BlockSpec((None, nblk, wd), lambda i, h, t: (i, 0, h)),
        ],
        out_specs=qspec,
        out_shape=jax.ShapeDtypeStruct((b, l, a), F32),
        scratch_shapes=[
            pltpu.VMEM((nh, nblk, blk), F32), pltpu.VMEM((nh, blk, LANES), BF16),
            pltpu.VMEM((nh, VT_ROWS, blk), F32),
        ],
        compiler_params=_cparams("parallel", "parallel", "arbitrary"),
        name="moba_prompt",
    )(q, qs, kb, vt, km)


def _in_glu_kernel(x_ref, g_ref, wa_ref, wg_ref, wz_ref, glu_ref, sz_ref):
    xn = _rms(x_ref[...], g_ref[...]).astype(BF16)
    a = jnp.dot(xn, wa_ref[...], preferred_element_type=F32)
    gb = jnp.dot(xn, wg_ref[...], preferred_element_type=F32)
    z = jnp.dot(xn, wz_ref[...], preferred_element_type=F32)
    glu_ref[...] = a * jax.nn.sigmoid(gb)
    sz_ref[...] = _silu(z)


def _in_glu(x, g, w_in, *, ch=512):
    n, d = x.shape
    e = w_in.shape[1] // 3
    nc = e // ch
    wspec = lambda k: pl.BlockSpec((d, ch), lambda c: (0, c + k * nc))
    ospec = pl.BlockSpec((n, ch), lambda c: (0, c))
    return pl.pallas_call(
        _in_glu_kernel,
        grid=(nc,),
        in_specs=[_full((n, d)), _full((1, d)), wspec(0), wspec(1), wspec(2)],
        out_specs=[ospec, ospec],
        out_shape=[jax.ShapeDtypeStruct((n, e), F32)] * 2,
        compiler_params=_cparams("parallel"), name="in_glu_sample",
    )(x, g.reshape(1, d), w_in, w_in, w_in)


def _dec_conv_kernel(st_ref, glu_ref, sz_ref, wdw_ref, bdw_ref, lng_ref, lnb_ref,
                     y_ref, ns_ref, conv_ref, *, bt):
    nh = CONV_W - 1
    w_hist = wdw_ref[0:nh, :]
    for i in range(bt):
        conv_ref[i:i + 1, :] = jnp.sum(st_ref[i] * w_hist, axis=0, keepdims=True)
        ns_ref[i, 0:nh - 1, :] = st_ref[i, 1:nh, :]
        ns_ref[i, nh - 1:nh, :] = glu_ref[i:i + 1, :]
    conv = conv_ref[...] + glu_ref[...] * wdw_ref[nh:nh + 1, :] + bdw_ref[...]
    mu = jnp.mean(conv, axis=-1, keepdims=True)
    d = conv - mu
    yn = d * lax.rsqrt(jnp.mean(d * d, axis=-1, keepdims=True) + EPS) * lng_ref[...] + lnb_ref[...]
    y_ref[...] = (_silu(yn) * sz_ref[...]).astype(BF16)


def _dec_conv(state, glu, sz, w_dw, b_dw, ln_g, ln_b, *, bt=16):
    n, nh, e = state.shape
    assert n % bt == 0 and nh == CONV_W - 1
    rspec = pl.BlockSpec((bt, e), lambda i: (i, 0))
    sspec = pl.BlockSpec((bt, nh, e), lambda i: (i, 0, 0))
    row = lambda w: w.reshape(1, e)
    return pl.pallas_call(
        functools.partial(_dec_conv_kernel, bt=bt),
        grid=(n // bt,),
        in_specs=[sspec, rspec, rspec, _full((CONV_W, e)), _full((1, e)), _full((1, e)), _full((1, e))],
        out_specs=[rspec, sspec],
        out_shape=[jax.ShapeDtypeStruct((n, e), BF16), jax.ShapeDtypeStruct((n, nh, e), F32)],
        scratch_shapes=[pltpu.VMEM((bt, e), F32)],
        compiler_params=_cparams("parallel"), name="conv_sample",
    )(state, glu, sz, w_dw, row(b_dw), row(ln_g), row(ln_b))


def _res_proj_kernel(y_ref, x_ref, w_ref, o_ref):
    o_ref[...] = x_ref[...] + jnp.dot(y_ref[...], w_ref[...], preferred_element_type=F32)


def _res_proj(y, x, w):
    n, d = x.shape
    e = y.shape[1]
    return pl.pallas_call(
        _res_proj_kernel, grid=(1,),
        in_specs=[_full((n, e)), _full((n, d)), _full((e, d))],
        out_specs=_full((n, d)), out_shape=jax.ShapeDtypeStruct((n, d), F32),
        compiler_params=_cparams("arbitrary"), name="res_proj_sample",
    )(y, x, w)


def _head_rows(shape):
    r = lax.broadcasted_iota(jnp.int32, shape, 0)
    c = lax.broadcasted_iota(jnp.int32, shape, 1)
    return r == (c >> 6)


def _dec_scores_kernel(pt_ref, qt_ref, *refs, pg, nh):
    del pt_ref
    k_refs, sc_ref = refs[:pg], refs[pg]
    hd, page = k_refs[0].shape[1:]
    qt = qt_ref[...]
    qcols = [jnp.broadcast_to(qt[:, h:h + 1], (hd, page)) for h in range(nh)]
    row_id = lax.broadcasted_iota(jnp.int32, (nh, page), 0)
    for p in range(pg):
        s = jnp.zeros((nh, page), F32)
        for h in range(nh):
            s = jnp.where(row_id == h, jnp.sum(k_refs[p][h] * qcols[h], axis=0, keepdims=True), s)
        sc_ref[p] = s * (HEAD_DIM ** -0.5)


def _dec_scores(page_table, qt, cache_t, *, pg=16):
    bd, npg = page_table.shape
    _, nh, hd, page = cache_t.shape
    assert npg % pg == 0 and page == LANES and hd % SUBLANES == 0
    kspec = lambda p: pl.BlockSpec((None, nh, hd, page), lambda b, j, pt: (pt[b, j * pg + p], 0, 0, 0))
    grid_spec = pltpu.PrefetchScalarGridSpec(
        num_scalar_prefetch=1, grid=(bd, npg // pg),
        in_specs=[pl.BlockSpec((None, hd, nh), lambda b, j, pt: (b, 0, 0))] + [kspec(p) for p in range(pg)],
        out_specs=pl.BlockSpec((None, pg, nh, page), lambda b, j, pt: (b, j, 0, 0)),
    )
    return pl.pallas_call(
        functools.partial(_dec_scores_kernel, pg=pg, nh=nh),
        grid_spec=grid_spec,
        out_shape=jax.ShapeDtypeStruct((bd, npg, nh, page), F32),
        compiler_params=_cparams("parallel", "arbitrary"), name="scores_sample",
    )(page_table, qt, *([cache_t] * pg))


def _dec_probs(sc_ref, q, kn, hrows, nblk, ppb):
    nh = hrows.shape[0]
    g = [functools.reduce(jnp.add, [jnp.sum(sc_ref[n * ppb + r], axis=-1, keepdims=True) for r in range(ppb)])
         for n in range(nblk)]
    sel = [jnp.zeros((nh, 1), F32) for _ in range(nblk)]
    for _ in range(min(MOBA_TOPK, nblk)):
        m = functools.reduce(jnp.maximum, g)
        idx = functools.reduce(jnp.minimum, [jnp.where(g[n] == m, n, nblk) for n in range(nblk)])
        for n in range(nblk):
            hit = idx == n
            sel[n] = jnp.where(hit, 1.0, sel[n])
            g[n] = jnp.where(hit, NEG_INF, g[n])
    s_own = jnp.sum(jnp.where(hrows, kn * q, 0.0), axis=-1, keepdims=True) * (HEAD_DIM ** -0.5)
    m = s_own
    sc = []
    for pgi in range(nblk * ppb):
        s = jnp.where(sel[pgi // ppb] > 0.0, sc_ref[pgi], NEG_INF)
        sc.append(s)
        m = jnp.maximum(m, jnp.max(s, axis=-1, keepdims=True))
    p_own = jnp.exp(s_own - m)
    l = p_own
    for pgi in range(nblk * ppb):
        sc[pgi] = jnp.exp(sc[pgi] - m)
        l = l + jnp.sum(sc[pgi], axis=-1, keepdims=True)
    inv = 1.0 / l
    return [p * inv for p in sc], p_own * inv


def _dec_pv_kernel(pt_ref, sc_ref, q_ref, kn_ref, vnt_ref, *refs, pg, nh):
    del pt_ref
    v_refs, (o_ref, p_ref, pown_ref, acc_ref) = refs[:pg], refs[pg:]
    j = pl.program_id(1)
    hd, page = v_refs[0].shape[1:]
    a = q_ref.shape[-1]

    @pl.when(j == 0)
    def _():
        nblk = sc_ref.shape[0] * page // MOBA_BLOCK
        ps, p_own = _dec_probs(sc_ref, q_ref[...], kn_ref[...], _head_rows((nh, a)), nblk, MOBA_BLOCK // page)
        for pgi, p in enumerate(ps):
            p_ref[pgi] = p
        pown_ref[...] = jnp.broadcast_to(p_own, pown_ref.shape)
        acc_ref[...] = jnp.zeros_like(acc_ref)

    for h in range(nh):
        acc = acc_ref[h]
        for p in range(pg):
            acc = acc + v_refs[p][h] * p_ref[j * pg + p, h:h + 1, :]
        acc_ref[h] = acc

    @pl.when(j == pl.num_programs(1) - 1)
    def _():
        lane_id = lax.broadcasted_iota(jnp.int32, (hd, nh), 1)
        out = jnp.zeros((hd, nh), F32)
        for h in range(nh):
            col = jnp.sum(acc_ref[h], axis=-1, keepdims=True) + pown_ref[h:h + 1, 0:1] * vnt_ref[:, h:h + 1]
            out = jnp.where(lane_id == h, col, out)
        o_ref[...] = out


def _dec_pv(page_table, scores, q, k_new, v_new_t, cache_t, *, pg=16):
    bd, npg = page_table.shape
    _, nh, hd, page = cache_t.shape
    a = nh * hd
    assert npg % pg == 0 and (npg * page) % MOBA_BLOCK == 0 and MOBA_BLOCK % page == 0
    vspec = lambda p: pl.BlockSpec((None, nh, hd, page), lambda b, j, pt: (pt[b, j * pg + p], 0, 0, 0))
    rowspec = pl.BlockSpec((None, 1, a), lambda b, j, pt: (b, 0, 0))
    colspec = pl.BlockSpec((None, hd, nh), lambda b, j, pt: (b, 0, 0))
    grid_spec = pltpu.PrefetchScalarGridSpec(
        num_scalar_prefetch=1, grid=(bd, npg // pg),
        in_specs=[pl.BlockSpec((None, npg, nh, page), lambda b, j, pt: (b, 0, 0, 0)),
                  rowspec, rowspec, colspec] + [vspec(p) for p in range(pg)],
        out_specs=colspec,
        scratch_shapes=[pltpu.VMEM((npg, nh, page), F32), pltpu.VMEM((nh, LANES), F32),
                        pltpu.VMEM((nh, hd, page), F32)],
    )
    return pl.pallas_call(
        functools.partial(_dec_pv_kernel, pg=pg, nh=nh),
        grid_spec=grid_spec,
        out_shape=jax.ShapeDtypeStruct((bd, hd, nh), F32),
        compiler_params=_cparams("parallel", "arbitrary"), name="pv_sample",
    )(page_table, scores, q, k_new, v_new_t, *([cache_t] * pg))


def kernel(x_prompt, x_sample, state_conv, cache_k, cache_v, page_table, g_a, w_in_a, w_dw, b_dw, ln_g_a, ln_b_a, w_out_a, g_kv, w_k, w_v, g_b, w_in_b, w_out_b, g_final):
    n_a, n_b = w_in_a.shape[0], w_in_b.shape[0]
    bp, lp, d = x_prompt.shape
    bd, ld, _ = x_sample.shape
    n_pool, page, nh, hd = cache_k.shape
    a = nh * hd
    nhist = CONV_W - 1
    assert ld == 1 and hd == HEAD_DIM and lp >= nhist
    assert (page_table.shape[1] * page) % MOBA_BLOCK == 0

    w_in_a_h, w_out_a_h = w_in_a.astype(BF16), w_out_a.astype(BF16)
    w_k_h, w_v_h, w_vt_h = w_k.astype(BF16), w_v.astype(BF16), w_v.T.astype(BF16)
    w_in_b_h, w_out_b_h = w_in_b.astype(BF16), w_out_b.astype(BF16)

    x = x_prompt
    bufs = []
    for i in range(n_a):
        x, buf = _a_layer_prompt(x, g_a[i], w_in_a_h[i], w_dw[i], b_dw[i], ln_g_a[i], ln_b_a[i], w_out_a_h[i])
        bufs.append(buf[:, HALO - nhist:, :])
    k_p, v_p, kb, vt, km = _kv_proj(x, g_kv, w_k_h, w_v_h, w_vt_h)
    km = km.reshape(bp, lp // MOBA_BLOCK, a)
    for j in range(n_b):
        q, qs, sz = _qz_proj(x, g_b[j], w_in_b_h[j], tm=512)
        o = _moba_prompt(q, qs, kb, vt, km)
        x = _out_proj(o, sz, x, w_out_b_h[j], g_final, final=(j == n_b - 1), tm=512)
    y_prompt = x
    conv_prompt = jnp.stack(bufs)

    xs = x_sample.reshape(bd, d)
    new_states = []
    for i in range(n_a):
        glu, sz = _in_glu(xs, g_a[i], w_in_a_h[i])
        y, ns = _dec_conv(state_conv[i], glu, sz, w_dw[i], b_dw[i], ln_g_a[i], ln_b_a[i])
        xs = _res_proj(y, xs, w_out_a_h[i])
        new_states.append(ns)
    k_s, v_s = _kv_proj(xs.reshape(1, bd, d), g_kv, w_k_h, w_v_h)
    ck_t = jnp.transpose(cache_k, (0, 2, 3, 1))
    cv_t = jnp.transpose(cache_v, (0, 2, 3, 1))
    heads_last = lambda r: jnp.transpose(r.reshape(bd, nh, hd), (0, 2, 1))
    k_row, v_new_t = k_s.reshape(bd, 1, a), heads_last(v_s)
    xs3 = xs.reshape(1, bd, d)
    for j in range(n_b):
        q, _, sz = _qz_proj(xs3, g_b[j], w_in_b_h[j], tm=bd)
        scores = _dec_scores(page_table, heads_last(q), ck_t)
        o_t = _dec_pv(page_table, scores, q.reshape(bd, 1, a), k_row, v_new_t, cv_t)
        o = jnp.transpose(o_t, (0, 2, 1)).reshape(1, bd, a)
        xs3 = _out_proj(o, sz, xs3, w_out_b_h[j], g_final, final=(j == n_b - 1), tm=bd)
    y_sample = xs3.reshape(bd, 1, d)

    return (y_prompt, y_sample,
            k_p.reshape(bp, lp, nh, hd), v_p.reshape(bp, lp, nh, hd), conv_prompt,
            k_s.reshape(bd, 1, nh, hd), v_s.reshape(bd, 1, nh, hd), jnp.stack(new_states))
```

```python
import functools

import jax
import jax.numpy as jnp
from jax import lax
from jax.experimental import pallas as pl
from jax.experimental.pallas import tpu as pltpu

F32 = jnp.float32
BF16 = jnp.bfloat16

EPS = 1e-6
CONV_W = 31
HALO = 32
HEAD_DIM = 64
VT_ROWS = HEAD_DIM + 16
MOBA_BLOCK = 256
MOBA_TOPK = 3
LANES = 128
SUBLANES = 8
VMEM_LIMIT = 56 * 1024 * 1024
NEG_INF = float("-inf")

_NT = (((1,), (1,)), ((), ()))


def _cparams(*sem):
    return pltpu.CompilerParams(dimension_semantics=sem, vmem_limit_bytes=VMEM_LIMIT)


def _rms(x, g):
    return x * lax.rsqrt(jnp.mean(x * x, axis=-1, keepdims=True) + EPS) * g


def _silu(x):
    return x * jax.nn.sigmoid(x)


def _full(shape):
    return pl.BlockSpec(shape, lambda *_: (0,) * len(shape))


def _conv_tap_groups():
    groups = {}
    for j in range(CONV_W):
        off = HALO - (CONV_W - 1) + j
        groups.setdefault(off % SUBLANES, []).append((j, off - off % SUBLANES))
    return groups


def _a_layer_kernel(x_ref, g_ref, win_ref, wdw_ref, bdw_ref, lng_ref, lnb_ref, wout_ref,
                    o_ref, buf_ref, s_ref, sz_ref, xn_ref, y_ref, wb_ref, *, tm, e, rc, ch):
    t = pl.program_id(1)

    @pl.when((pl.program_id(0) == 0) & (t == 0))
    def _():
        for j in range(CONV_W):
            wb_ref[j] = jnp.broadcast_to(wdw_ref[j:j + 1, :], (SUBLANES, e))

    @pl.when(t == 0)
    def _():
        s_ref[0:HALO, :] = jnp.zeros((HALO, e), F32)

    xn_ref[...] = _rms(x_ref[...], g_ref[...]).astype(BF16)
    for c in range(e // ch):
        xn = xn_ref[...]
        a = jnp.dot(xn, win_ref[:, c * ch:(c + 1) * ch], preferred_element_type=F32)
        gb = jnp.dot(xn, win_ref[:, e + c * ch:e + (c + 1) * ch], preferred_element_type=F32)
        z = jnp.dot(xn, win_ref[:, 2 * e + c * ch:2 * e + (c + 1) * ch], preferred_element_type=F32)
        s_ref[HALO:HALO + tm, c * ch:(c + 1) * ch] = a * jax.nn.sigmoid(gb)
        sz_ref[:, c * ch:(c + 1) * ch] = _silu(z)

    @pl.when(t == pl.num_programs(1) - 1)
    def _():
        buf_ref[...] = s_ref[tm:tm + HALO, :]

    groups = _conv_tap_groups()
    ncb = e // LANES
    nv = rc // SUBLANES

    def row_chunk(ci, _):
        r0 = pl.multiple_of(ci * rc, rc)

        def conv_cb(cb, s1):
            c0 = pl.multiple_of(cb * LANES, LANES)
            w3 = s_ref[pl.ds(r0, rc + HALO), pl.ds(c0, LANES)].reshape(nv + HALO // SUBLANES, SUBLANES, LANES)
            acc = jnp.broadcast_to(bdw_ref[:, pl.ds(c0, LANES)], (rc, LANES))
            for sh, taps in sorted(groups.items()):
                n = nv if sh == 0 else nv + 1
                u = None
                for j, a0 in taps:
                    v0 = a0 // SUBLANES
                    term = wb_ref[j, :, pl.ds(c0, LANES)][None] * w3[v0:v0 + n]
                    u = term if u is None else u + term
                u = u.reshape(n * SUBLANES, LANES)
                acc = acc + (u if sh == 0 else u[sh:sh + rc])
            s_ref[pl.ds(r0, rc), pl.ds(c0, LANES)] = acc
            return s1 + acc

        s1 = lax.fori_loop(0, ncb, conv_cb, jnp.zeros((rc, LANES), F32))
        mu = jnp.sum(s1, axis=-1, keepdims=True) * (1.0 / e)

        def var_cb(cb, s2):
            c0 = pl.multiple_of(cb * LANES, LANES)
            d = s_ref[pl.ds(r0, rc), pl.ds(c0, LANES)] - mu
            return s2 + d * d

        s2 = lax.fori_loop(0, ncb, var_cb, jnp.zeros((rc, LANES), F32))
        rstd = lax.rsqrt(jnp.sum(s2, axis=-1, keepdims=True) * (1.0 / e) + EPS)

        def norm_cb(cb, _):
            c0 = pl.multiple_of(cb * LANES, LANES)
            cv = s_ref[pl.ds(r0, rc), pl.ds(c0, LANES)]
            yn = (cv - mu) * rstd * lng_ref[:, pl.ds(c0, LANES)] + lnb_ref[:, pl.ds(c0, LANES)]
            y = _silu(yn) * sz_ref[pl.ds(r0, rc), pl.ds(c0, LANES)]
            y_ref[pl.ds(r0, rc), pl.ds(c0, LANES)] = y.astype(BF16)
            return 0

        lax.fori_loop(0, ncb, norm_cb, 0)
        return 0

    lax.fori_loop(0, tm // rc, row_chunk, 0)
    s_ref[0:HALO, :] = s_ref[tm:tm + HALO, :]
    o_ref[...] = x_ref[...] + jnp.dot(y_ref[...], wout_ref[...], preferred_element_type=F32)


def _a_layer_prompt(x, g, w_in, w_dw, b_dw, ln_g, ln_b, w_out, *, tm=256, rc=128, ch=512):
    b, l, d = x.shape
    e = w_dw.shape[-1]
    assert l % tm == 0 and tm % rc == 0 and tm >= HALO and e % ch == 0
    kern = functools.partial(_a_layer_kernel, tm=tm, e=e, rc=rc, ch=ch)
    row = lambda w: w.reshape(1, -1)
    return pl.pallas_call(
        kern,
        grid=(b, l // tm),
        in_specs=[
            pl.BlockSpec((None, tm, d), lambda i, t: (i, t, 0)),
            _full((1, d)), _full((d, 3 * e)), _full((CONV_W, e)),
            _full((1, e)), _full((1, e)), _full((1, e)), _full((e, d)),
        ],
        out_specs=[
            pl.BlockSpec((None, tm, d), lambda i, t: (i, t, 0)),
            pl.BlockSpec((None, HALO, e), lambda i, t: (i, 0, 0)),
        ],
        out_shape=[jax.ShapeDtypeStruct((b, l, d), F32), jax.ShapeDtypeStruct((b, HALO, e), F32)],
        scratch_shapes=[
            pltpu.VMEM((HALO + tm, e), F32), pltpu.VMEM((tm, e), F32),
            pltpu.VMEM((tm, d), BF16), pltpu.VMEM((tm, e), BF16),
            pltpu.VMEM((CONV_W, SUBLANES, e), F32),
        ],
        compiler_params=_cparams("arbitrary", "arbitrary"),
        name="a_layer_prompt",
    )(x, row(g), w_in, w_dw, row(b_dw), row(ln_g), row(ln_b), w_out)


def _kv_kernel(*refs, extras):
    if extras:
        x_ref, g_ref, wk_ref, wv_ref, wvt_ref, k_ref, v_ref, kb_ref, vt_ref, km_ref = refs
    else:
        x_ref, g_ref, wk_ref, wv_ref, k_ref, v_ref = refs
    hk = _rms(x_ref[...], g_ref[...]).astype(BF16)
    k = jnp.dot(hk, wk_ref[...], preferred_element_type=F32)
    k_ref[...] = k
    v_ref[...] = jnp.dot(hk, wv_ref[...], preferred_element_type=F32)
    if extras:
        kb_ref[...] = k.astype(BF16)
        nh = vt_ref.shape[0]
        vt = lax.dot_general(wvt_ref[...], hk, _NT, preferred_element_type=F32)
        vt_ref[:, 0:HEAD_DIM, :] = vt.reshape(nh, HEAD_DIM, vt.shape[-1]).astype(BF16)
        vt_ref[:, HEAD_DIM:, :] = jnp.ones((nh, VT_ROWS - HEAD_DIM, vt.shape[-1]), BF16)
        km_ref[...] = jnp.sum(k, axis=0, keepdims=True) * (1.0 / MOBA_BLOCK)


def _kv_proj(x, g, w_k, w_v, w_vt=None):
    b, l, d = x.shape
    a = w_k.shape[1]
    extras = w_vt is not None
    tm = MOBA_BLOCK if extras else l
    assert l % tm == 0
    xspec = pl.BlockSpec((None, tm, d), lambda i, t: (i, t, 0))
    ospec = pl.BlockSpec((None, tm, a), lambda i, t: (i, t, 0))
    in_specs = [xspec, _full((1, d)), _full((d, a)), _full((d, a))]
    out_specs = [ospec, ospec]
    out_shape = [jax.ShapeDtypeStruct((b, l, a), F32)] * 2
    args = [x, g.reshape(1, d), w_k, w_v]
    if extras:
        in_specs.append(_full((a, d)))
        args.append(w_vt)
        nh = a // HEAD_DIM
        out_specs += [ospec, pl.BlockSpec((None, nh, VT_ROWS, tm), lambda i, t: (i, 0, 0, t)),
                      pl.BlockSpec((None, None, 1, a), lambda i, t: (i, t, 0, 0))]
        out_shape += [jax.ShapeDtypeStruct((b, l, a), BF16), jax.ShapeDtypeStruct((b, nh, VT_ROWS, l), BF16),
                      jax.ShapeDtypeStruct((b, l // tm, 1, a), F32)]
    return pl.pallas_call(
        functools.partial(_kv_kernel, extras=extras),
        grid=(b, l // tm), in_specs=in_specs, out_specs=out_specs, out_shape=out_shape,
        compiler_params=_cparams("parallel", "parallel"), name="kv_proj",
    )(*args)


def _qz_kernel(x_ref, g_ref, w_ref, q_ref, qs_ref, sz_ref, *, a):
    xn = _rms(x_ref[...], g_ref[...]).astype(BF16)
    q = jnp.dot(xn, w_ref[:, 0:a], preferred_element_type=F32)
    z = jnp.dot(xn, w_ref[:, a:2 * a], preferred_element_type=F32)
    q_ref[...] = q
    qs_ref[...] = (q * (HEAD_DIM ** -0.5)).astype(BF16)
    sz_ref[...] = _silu(z)


def _qz_proj(x, g, w_in, *, tm):
    b, l, d = x.shape
    a = w_in.shape[1] // 2
    assert l % tm == 0
    xspec = pl.BlockSpec((None, tm, d), lambda i, t: (i, t, 0))
    ospec = pl.BlockSpec((None, tm, a), lambda i, t: (i, t, 0))
    return pl.pallas_call(
        functools.partial(_qz_kernel, a=a),
        grid=(b, l // tm),
        in_specs=[xspec, _full((1, d)), _full((d, 2 * a))],
        out_specs=[ospec, ospec, ospec],
        out_shape=[jax.ShapeDtypeStruct((b, l, a), F32), jax.ShapeDtypeStruct((b, l, a), BF16),
                   jax.ShapeDtypeStruct((b, l, a), F32)],
        compiler_params=_cparams("parallel", "parallel"), name="qz_proj",
    )(x, g.reshape(1, d), w_in)


def _out_kernel(o_ref, sz_ref, x_ref, w_ref, gf_ref, y_ref, *, final):
    y = (o_ref[...] * sz_ref[...]).astype(BF16)
    xo = x_ref[...] + jnp.dot(y, w_ref[...], preferred_element_type=F32)
    y_ref[...] = _rms(xo, gf_ref[...]) if final else xo


def _out_proj(o, sz, x, w_out, g_final, *, final, tm):
    b, l, d = x.shape
    a = o.shape[-1]
    assert l % tm == 0
    aspec = pl.BlockSpec((None, tm, a), lambda i, t: (i, t, 0))
    xspec = pl.BlockSpec((None, tm, d), lambda i, t: (i, t, 0))
    return pl.pallas_call(
        functools.partial(_out_kernel, final=final),
        grid=(b, l // tm),
        in_specs=[aspec, aspec, xspec, _full((a, d)), _full((1, d))],
        out_specs=xspec,
        out_shape=jax.ShapeDtypeStruct((b, l, d), F32),
        compiler_params=_cparams("parallel", "parallel"), name="out_proj",
    )(o, sz, x, w_out, g_final.reshape(1, d))


def _moba_prompt_kernel(q_ref, qs_ref, k_ref, vt_ref, km_ref, o_ref, sel_ref, qh_ref, acc_ref, *, nblk, sb, nh):
    i = pl.program_id(2)
    blk = MOBA_BLOCK
    w = sb * blk
    lane = lax.broadcasted_iota(jnp.int32, (1, LANES), 1)
    n_iota = lax.broadcasted_iota(jnp.int32, (nblk, blk), 0)
    pair = lambda hh: slice((hh // 2) * LANES, (hh // 2 + 1) * LANES)
    for hh in range(nh):
        hmask = (lane >> 6) == hh % 2
        g = lax.dot_general(km_ref[:, pair(hh)], jnp.where(hmask, q_ref[:, pair(hh)], 0.0), _NT,
                            precision=lax.Precision.HIGHEST, preferred_element_type=F32)
        g = jnp.where(n_iota < i, g, NEG_INF)
        sel = jnp.zeros((nblk, blk), F32)
        for _ in range(MOBA_TOPK):
            m = jnp.max(g, axis=0, keepdims=True)
            idx = jnp.min(jnp.where(g == m, n_iota, nblk), axis=0, keepdims=True)
            hit = n_iota == idx
            sel = jnp.where(hit & (m > NEG_INF), 1.0, sel)
            g = jnp.where(hit, NEG_INF, g)
        sel_ref[hh] = sel
        qs = qs_ref[:, pair(hh)]
        qh_ref[hh] = jnp.where(hmask, qs, jnp.zeros_like(qs))

    def scores(start, rows):
        return [lax.dot_general(k_ref[pl.ds(start, rows), pair(hh)], qh_ref[hh], _NT, preferred_element_type=F32)
                for hh in range(nh)]

    k_iota = lax.broadcasted_iota(jnp.int32, (blk, blk), 0)
    q_iota = lax.broadcasted_iota(jnp.int32, (blk, blk), 1)
    own = pl.multiple_of(i * blk, blk)
    sts = scores(own, blk)
    maxes = []
    for hh in range(nh):
        st = jnp.where(k_iota <= q_iota, sts[hh], NEG_INF)
        m = jnp.max(st, axis=0, keepdims=True)
        maxes.append(m)
        acc_ref[hh] = jnp.dot(vt_ref[hh, :, pl.ds(own, blk)], jnp.exp(st - m).astype(BF16),
                              preferred_element_type=F32)

    def body(t, maxes):
        start = pl.multiple_of(t * w, w)
        sts = scores(start, w)
        out = []
        for hh in range(nh):
            m = maxes[hh]
            st = sts[hh]
            sels = [sel_ref[hh, pl.ds(t * sb + j, 1), :] > 0.0 for j in range(sb)]
            m_new = m
            for j in range(sb):
                mj = jnp.max(st[j * blk:(j + 1) * blk], axis=0, keepdims=True)
                m_new = jnp.maximum(m_new, jnp.where(sels[j], mj, NEG_INF))
            ps = [jnp.where(sels[j], jnp.exp(st[j * blk:(j + 1) * blk] - m_new), 0.0).astype(BF16)
                  for j in range(sb)]
            acc_ref[hh] = jnp.exp(m - m_new) * acc_ref[hh] + jnp.dot(
                vt_ref[hh, :, pl.ds(start, w)], jnp.concatenate(ps, axis=0), preferred_element_type=F32)
            out.append(m_new)
        return tuple(out)

    lax.fori_loop(0, (i + sb - 1) // sb, body, tuple(maxes))
    ot = jnp.concatenate([acc_ref[hh, 0:HEAD_DIM, :] / acc_ref[hh, HEAD_DIM:HEAD_DIM + 1, :] for hh in range(nh)],
                         axis=0)
    o_ref[...] = ot.T


def _moba_prompt(q, qs, kb, vt, km, *, sb=2, nh=8):
    b, l, a = q.shape
    blk = MOBA_BLOCK
    nblk = l // blk
    wd = nh * HEAD_DIM
    assert l % blk == 0 and nblk % sb == 0 and a % wd == 0 and wd % LANES == 0 and LANES == 2 * HEAD_DIM
    qspec = pl.BlockSpec((None, blk, wd), lambda i, h, t: (i, t, h))
    return pl.pallas_call(
        functools.partial(_moba_prompt_kernel, nblk=nblk, sb=sb, nh=nh),
        grid=(b, a // wd, nblk),
        in_specs=[
            qspec, qspec,
            pl.BlockSpec((None, l, wd), lambda i, h, t: (i, 0, h)),
            pl.BlockSpec((None, nh, VT_ROWS, l), lambda i, h, t: (i, h, 0, 0)),
            pl.BlockSpec((None, nblk, wd), lambda i, h, t: (i, 0, h)),
        ],
        out_specs=qspec,
        out_shape=jax.ShapeDtypeStruct((b, l, a), F32),
        scratch_shapes=[
            pltpu.VMEM((nh, nblk, blk), F32), pltpu.VMEM((nh, blk, LANES), BF16),
            pltpu.VMEM((nh, VT_ROWS, blk), F32),
        ],
        compiler_params=_cparams("parallel", "parallel", "arbitrary"),
        name="moba_prompt",
    )(q, qs, kb, vt, km)


def _in_glu_kernel(x_ref, g_ref, wa_ref, wg_ref, wz_ref, glu_ref, sz_ref):
    xn = _rms(x_ref[...], g_ref[...]).astype(BF16)
    a = jnp.dot(xn, wa_ref[...], preferred_element_type=F32)
    gb = jnp.dot(xn, wg_ref[...], preferred_element_type=F32)
    z = jnp.dot(xn, wz_ref[...], preferred_element_type=F32)
    glu_ref[...] = a * jax.nn.sigmoid(gb)
    sz_ref[...] = _silu(z)


def _in_glu(x, g, w_in, *, ch=512):
    n, d = x.shape
    e = w_in.shape[1] // 3
    nc = e // ch
    wspec = lambda k: pl.BlockSpec((d, ch), lambda c: (0, c + k * nc))
    ospec = pl.BlockSpec((n, ch), lambda c: (0, c))
    return pl.pallas_call(
        _in_glu_kernel,
        grid=(nc,),
        in_specs=[_full((n, d)), _full((1, d)), wspec(0), wspec(1), wspec(2)],
        out_specs=[ospec, ospec],
        out_shape=[jax.ShapeDtypeStruct((n, e), F32)] * 2,
        compiler_params=_cparams("parallel"), name="in_glu_sample",
    )(x, g.reshape(1, d), w_in, w_in, w_in)


def _dec_conv_kernel(st_ref, glu_ref, sz_ref, wdw_ref, bdw_ref, lng_ref, lnb_ref, y_ref, ns_ref):
    nh = CONV_W - 1
    glu = glu_ref[...]
    conv = glu * wdw_ref[nh:nh + 1, :] + bdw_ref[...]
    for j in range(nh):
        conv = conv + st_ref[j] * wdw_ref[j:j + 1, :]
    for j in range(nh - 1):
        ns_ref[j] = st_ref[j + 1]
    ns_ref[nh - 1] = glu
    mu = jnp.mean(conv, axis=-1, keepdims=True)
    d = conv - mu
    yn = d * lax.rsqrt(jnp.mean(d * d, axis=-1, keepdims=True) + EPS) * lng_ref[...] + lnb_ref[...]
    y_ref[...] = (_silu(yn) * sz_ref[...]).astype(BF16)


def _dec_conv(state_t, layer, glu, sz, w_dw, b_dw, ln_g, ln_b, *, bt=16):
    _, nh, n, e = state_t.shape
    assert n % bt == 0 and nh == CONV_W - 1
    rspec = pl.BlockSpec((bt, e), lambda i: (i, 0))
    row = lambda w: w.reshape(1, e)
    return pl.pallas_call(
        _dec_conv_kernel,
        grid=(n // bt,),
        in_specs=[pl.BlockSpec((None, nh, bt, e), lambda i: (layer, 0, i, 0)), rspec, rspec,
                  _full((CONV_W, e)), _full((1, e)), _full((1, e)), _full((1, e))],
        out_specs=[rspec, pl.BlockSpec((nh, bt, e), lambda i: (0, i, 0))],
        out_shape=[jax.ShapeDtypeStruct((n, e), BF16), jax.ShapeDtypeStruct((nh, n, e), F32)],
        compiler_params=_cparams("parallel"), name="conv_sample",
    )(state_t, glu, sz, w_dw, row(b_dw), row(ln_g), row(ln_b))


def _res_proj_kernel(y_ref, x_ref, w_ref, o_ref):
    o_ref[...] = x_ref[...] + jnp.dot(y_ref[...], w_ref[...], preferred_element_type=F32)


def _res_proj(y, x, w):
    n, d = x.shape
    e = y.shape[1]
    return pl.pallas_call(
        _res_proj_kernel, grid=(1,),
        in_specs=[_full((n, e)), _full((n, d)), _full((e, d))],
        out_specs=_full((n, d)), out_shape=jax.ShapeDtypeStruct((n, d), F32),
        compiler_params=_cparams("arbitrary"), name="res_proj_sample",
    )(y, x, w)


def _head_rows(shape):
    r = lax.broadcasted_iota(jnp.int32, shape, 0)
    c = lax.broadcasted_iota(jnp.int32, shape, 1)
    return r == (c >> 6)


def _dec_scores_kernel(pt_ref, qt_ref, *refs, pg, nh):
    del pt_ref
    k_refs, sc_ref = refs[:pg], refs[pg]
    hd, page = k_refs[0].shape[1:]
    qt = qt_ref[...]
    qcols = [jnp.broadcast_to(qt[:, h:h + 1], (hd, page)) for h in range(nh)]
    row_id = lax.broadcasted_iota(jnp.int32, (nh, page), 0)
    for p in range(pg):
        s = jnp.zeros((nh, page), F32)
        for h in range(nh):
            s = jnp.where(row_id == h, jnp.sum(k_refs[p][h] * qcols[h], axis=0, keepdims=True), s)
        sc_ref[p] = s * (HEAD_DIM ** -0.5)


def _dec_scores(page_table, qt, cache_t, *, pg=16):
    bd, npg = page_table.shape
    _, nh, hd, page = cache_t.shape
    assert npg % pg == 0 and page == LANES and hd % SUBLANES == 0
    kspec = lambda p: pl.BlockSpec((None, nh, hd, page), lambda b, j, pt: (pt[b, j * pg + p], 0, 0, 0))
    grid_spec = pltpu.PrefetchScalarGridSpec(
        num_scalar_prefetch=1, grid=(bd, npg // pg),
        in_specs=[pl.BlockSpec((None, hd, nh), lambda b, j, pt: (b, 0, 0))] + [kspec(p) for p in range(pg)],
        out_specs=pl.BlockSpec((None, pg, nh, page), lambda b, j, pt: (b, j, 0, 0)),
    )
    return pl.pallas_call(
        functools.partial(_dec_scores_kernel, pg=pg, nh=nh),
        grid_spec=grid_spec,
        out_shape=jax.ShapeDtypeStruct((bd, npg, nh, page), F32),
        compiler_params=_cparams("parallel", "arbitrary"), name="scores_sample",
    )(page_table, qt, *([cache_t] * pg))


def _dec_probs(sc_ref, q, kn, hrows, nblk, ppb):
    nh = hrows.shape[0]
    g = [functools.reduce(jnp.add, [jnp.sum(sc_ref[n * ppb + r], axis=-1, keepdims=True) for r in range(ppb)])
         for n in range(nblk)]
    sel = [jnp.zeros((nh, 1), F32) for _ in range(nblk)]
    for _ in range(min(MOBA_TOPK, nblk)):
        m = functools.reduce(jnp.maximum, g)
        idx = functools.reduce(jnp.minimum, [jnp.where(g[n] == m, n, nblk) for n in range(nblk)])
        for n in range(nblk):
            hit = idx == n
            sel[n] = jnp.where(hit, 1.0, sel[n])
            g[n] = jnp.where(hit, NEG_INF, g[n])
    s_own = jnp.sum(jnp.where(hrows, kn * q, 0.0), axis=-1, keepdims=True) * (HEAD_DIM ** -0.5)
    m = s_own
    sc = []
    for pgi in range(nblk * ppb):
        s = jnp.where(sel[pgi // ppb] > 0.0, sc_ref[pgi], NEG_INF)
        sc.append(s)
        m = jnp.maximum(m, jnp.max(s, axis=-1, keepdims=True))
    p_own = jnp.exp(s_own - m)
    l = p_own
    for pgi in range(nblk * ppb):
        sc[pgi] = jnp.exp(sc[pgi] - m)
        l = l + jnp.sum(sc[pgi], axis=-1, keepdims=True)
    inv = 1.0 / l
    return [p * inv for p in sc], p_own * inv


def _dec_pv_kernel(pt_ref, sc_ref, q_ref, kn_ref, vnt_ref, *refs, pg, nh):
    del pt_ref
    v_refs, (o_ref, p_ref, pown_ref, acc_ref) = refs[:pg], refs[pg:]
    j = pl.program_id(1)
    hd, page = v_refs[0].shape[1:]
    a = q_ref.shape[-1]

    @pl.when(j == 0)
    def _():
        nblk = sc_ref.shape[0] * page // MOBA_BLOCK
        ps, p_own = _dec_probs(sc_ref, q_ref[...], kn_ref[...], _head_rows((nh, a)), nblk, MOBA_BLOCK // page)
        for pgi, p in enumerate(ps):
            p_ref[pgi] = p
        pown_ref[...] = jnp.broadcast_to(p_own, pown_ref.shape)
        acc_ref[...] = jnp.zeros_like(acc_ref)

    for h in range(nh):
        acc = acc_ref[h]
        for p in range(pg):
            acc = acc + v_refs[p][h] * p_ref[j * pg + p, h:h + 1, :]
        acc_ref[h] = acc

    @pl.when(j == pl.num_programs(1) - 1)
    def _():
        lane_id = lax.broadcasted_iota(jnp.int32, (hd, nh), 1)
        out = jnp.zeros((hd, nh), F32)
        for h in range(nh):
            col = jnp.sum(acc_ref[h], axis=-1, keepdims=True) + pown_ref[h:h + 1, 0:1] * vnt_ref[:, h:h + 1]
            out = jnp.where(lane_id == h, col, out)
        o_ref[...] = out


def _dec_pv(page_table, scores, q, k_new, v_new_t, cache_t, *, pg=16):
    bd, npg = page_table.shape
    _, nh, hd, page = cache_t.shape
    a = nh * hd
    assert npg % pg == 0 and (npg * page) % MOBA_BLOCK == 0 and MOBA_BLOCK % page == 0
    vspec = lambda p: pl.BlockSpec((None, nh, hd, page), lambda b, j, pt: (pt[b, j * pg + p], 0, 0, 0))
    rowspec = pl.BlockSpec((None, 1, a), lambda b, j, pt: (b, 0, 0))
    colspec = pl.BlockSpec((None, hd, nh), lambda b, j, pt: (b, 0, 0))
    grid_spec = pltpu.PrefetchScalarGridSpec(
        num_scalar_prefetch=1, grid=(bd, npg // pg),
        in_specs=[pl.BlockSpec((None, npg, nh, page), lambda b, j, pt: (b, 0, 0, 0)),
                  rowspec, rowspec, colspec] + [vspec(p) for p in range(pg)],
        out_specs=colspec,
        scratch_shapes=[pltpu.VMEM((npg, nh, page), F32), pltpu.VMEM((nh, LANES), F32),
                        pltpu.VMEM((nh, hd, page), F32)],
    )
    return pl.pallas_call(
        functools.partial(_dec_pv_kernel, pg=pg, nh=nh),
        grid_spec=grid_spec,
        out_shape=jax.ShapeDtypeStruct((bd, hd, nh), F32),
        compiler_params=_cparams("parallel", "arbitrary"), name="pv_sample",
    )(page_table, scores, q, k_new, v_new_t, *([cache_t] * pg))


def kernel(x_prompt, x_sample, state_conv, cache_k, cache_v, page_table, g_a, w_in_a, w_dw, b_dw, ln_g_a, ln_b_a, w_out_a, g_kv, w_k, w_v, g_b, w_in_b, w_out_b, g_final):
    n_a, n_b = w_in_a.shape[0], w_in_b.shape[0]
    bp, lp, d = x_prompt.shape
    bd, ld, _ = x_sample.shape
    n_pool, page, nh, hd = cache_k.shape
    a = nh * hd
    nhist = CONV_W - 1
    assert ld == 1 and hd == HEAD_DIM and lp >= nhist
    assert (page_table.shape[1] * page) % MOBA_BLOCK == 0

    w_in_a_h, w_out_a_h = w_in_a.astype(BF16), w_out_a.astype(BF16)
    w_k_h, w_v_h, w_vt_h = w_k.astype(BF16), w_v.astype(BF16), w_v.T.astype(BF16)
    w_in_b_h, w_out_b_h = w_in_b.astype(BF16), w_out_b.astype(BF16)

    x = x_prompt
    bufs = []
    for i in range(n_a):
        x, buf = _a_layer_prompt(x, g_a[i], w_in_a_h[i], w_dw[i], b_dw[i], ln_g_a[i], ln_b_a[i], w_out_a_h[i])
        bufs.append(buf[:, HALO - nhist:, :])
    k_p, v_p, kb, vt, km = _kv_proj(x, g_kv, w_k_h, w_v_h, w_vt_h)
    km = km.reshape(bp, lp // MOBA_BLOCK, a)
    for j in range(n_b):
        q, qs, sz = _qz_proj(x, g_b[j], w_in_b_h[j], tm=512)
        o = _moba_prompt(q, qs, kb, vt, km)
        x = _out_proj(o, sz, x, w_out_b_h[j], g_final, final=(j == n_b - 1), tm=512)
    y_prompt = x
    conv_prompt = jnp.stack(bufs)

    xs = x_sample.reshape(bd, d)
    new_states = []
    state_t = jnp.transpose(state_conv, (0, 2, 1, 3))
    for i in range(n_a):
        glu, sz = _in_glu(xs, g_a[i], w_in_a_h[i])
        y, ns = _dec_conv(state_t, i, glu, sz, w_dw[i], b_dw[i], ln_g_a[i], ln_b_a[i])
        xs = _res_proj(y, xs, w_out_a_h[i])
        new_states.append(ns)
    conv_sample = jnp.transpose(jnp.stack(new_states), (0, 2, 1, 3))
    k_s, v_s = _kv_proj(xs.reshape(1, bd, d), g_kv, w_k_h, w_v_h)
    ck_t = jnp.transpose(cache_k, (0, 2, 3, 1))
    cv_t = jnp.transpose(cache_v, (0, 2, 3, 1))
    heads_last = lambda r: jnp.transpose(r.reshape(bd, nh, hd), (0, 2, 1))
    k_row, v_new_t = k_s.reshape(bd, 1, a), heads_last(v_s)
    xs3 = xs.reshape(1, bd, d)
    for j in range(n_b):
        q, _, sz = _qz_proj(xs3, g_b[j], w_in_b_h[j], tm=bd)
        scores = _dec_scores(page_table, heads_last(q), ck_t)
        o_t = _dec_pv(page_table, scores, q.reshape(bd, 1, a), k_row, v_new_t, cv_t)
        o = jnp.transpose(o_t, (0, 2, 1)).reshape(1, bd, a)
        xs3 = _out_proj(o, sz, xs3, w_out_b_h[j], g_final, final=(j == n_b - 1), tm=bd)
    y_sample = xs3.reshape(bd, 1, d)

    return (y_prompt, y_sample,
            k_p.reshape(bp, lp, nh, hd), v_p.reshape(bp, lp, nh, hd), conv_prompt,
            k_s.reshape(bd, 1, nh, hd), v_s.reshape(bd, 1, nh, hd), conv_sample)
```
